```python
import math
import jax, jax.numpy as jnp
from jax import lax
import numpy as np


D_MODEL = 1024
BATCH = 8
SEQ = 2048
DEPTH = 1
DEC_BATCH = 128
DEC_SEQ = 1
PAST_LEN = 16384
PAGE_SIZE = 128

D_MIX = D_MODEL
D_REC = D_MIX // 2
D_CONV = D_MIX - D_REC
REC_HEADS = 8
REC_HEAD_DIM = D_REC // REC_HEADS
REC_CONV_W = 4
RG_C = 8.0
CONF_CONV_W = 31
N_MEM = 256
XA_HEADS = 4
XA_HEAD_DIM = D_MODEL // XA_HEADS
N_EXPERTS = 64
TOP_K = 8
N_GROUPS = 8
TOPK_GROUPS = 4
D_EXPERT = D_MODEL // 4
D_SHARED = D_EXPERT
ROUTED_SCALE = 2.5
EXPERT_BLOCK = 128
LN_EPS = 1e-5
ALPHA = (2 * DEPTH) ** 0.25
BETA = (8 * DEPTH) ** -0.25

kernel_name = 'hymba_griffin_conformer_moe_decoder_step'


def layer_norm(x, g, b):
    xf = x.astype(jnp.float32)
    mu = jnp.mean(xf, axis=-1, keepdims=True)
    var = jnp.mean(jnp.square(xf - mu), axis=-1, keepdims=True)
    return ((xf - mu) * lax.rsqrt(var + LN_EPS) * g.astype(jnp.float32) + b.astype(jnp.float32)).astype(x.dtype)


def causal_dwconv(u, buf, w, b):
    ext = jnp.concatenate([buf.astype(u.dtype), u], axis=1)
    y = lax.conv_general_dilated(ext, w[:, None, :].astype(u.dtype), window_strides=(1,), padding='VALID',
                                 dimension_numbers=('NWC', 'WIO', 'NWC'), feature_group_count=u.shape[-1])
    return y + b.astype(u.dtype), ext[:, ext.shape[1] - (w.shape[0] - 1):]


def block_diag(u, w, b):
    n, l, _ = u.shape
    uh = u.reshape(n, l, REC_HEADS, REC_HEAD_DIM)
    return jnp.einsum('nlhi,hij->nlhj', uh, w.astype(u.dtype)).reshape(n, l, -1) + b.astype(u.dtype)


def rglru(u, h0, w_a, b_a, w_x, b_x, lam):
    uf = u.astype(jnp.float32)
    r = jax.nn.sigmoid(block_diag(uf, w_a.astype(jnp.float32), b_a.astype(jnp.float32)))
    i = jax.nn.sigmoid(block_diag(uf, w_x.astype(jnp.float32), b_x.astype(jnp.float32)))
    log_a = -RG_C * r * jax.nn.softplus(-lam.astype(jnp.float32))
    a = jnp.exp(log_a)
    bterm = jnp.sqrt(-jnp.expm1(2.0 * log_a)) * (i * uf)
    bterm = bterm.at[:, 0].add(a[:, 0] * h0.astype(jnp.float32))

    def combine(left, right):
        a1, b1 = left
        a2, b2 = right
        return a1 * a2, a2 * b1 + b2

    _, h = lax.associative_scan(combine, (a, bterm), axis=1)
    return h, h[:, -1]


def cross_attn(x, k, v, w_q, w_o):
    n, l, _ = x.shape
    q = (x @ w_q).reshape(n, l, XA_HEADS, XA_HEAD_DIM)
    s = jnp.einsum('nlhd,nmhd->nhlm', q, k.astype(x.dtype)).astype(jnp.float32) * (XA_HEAD_DIM ** -0.5)
    p = jax.nn.softmax(s, axis=-1).astype(x.dtype)
    o = jnp.einsum('nhlm,nmhd->nlhd', p, v.astype(x.dtype)).reshape(n, l, XA_HEADS * XA_HEAD_DIM)
    return o @ w_o


def swiglu(x, w1, w3, w2):
    return (jax.nn.silu(x @ w1) * (x @ w3)) @ w2


def routed_experts(xf, idx, gates, w1_e, w3_e, w2_e):
    t, d = xf.shape
    n_assign = t * TOP_K
    flat_e = idx.reshape(-1)
    order = jnp.argsort(flat_e, stable=True)
    e_sorted = flat_e[order]
    tok_sorted = (order // TOP_K).astype(jnp.int32)
    gate_sorted = gates.reshape(-1)[order]
    counts = jnp.bincount(flat_e, length=N_EXPERTS)
    padded = (counts + EXPERT_BLOCK - 1) // EXPERT_BLOCK * EXPERT_BLOCK
    pad_end = jnp.cumsum(padded)
    pad_start = pad_end - padded
    start = jnp.cumsum(counts) - counts
    dest = pad_start[e_sorted] + jnp.arange(n_assign, dtype=jnp.int32) - start[e_sorted]
    n_blocks = (n_assign + N_EXPERTS * (EXPERT_BLOCK - 1) + EXPERT_BLOCK - 1) // EXPERT_BLOCK
    n_rows = n_blocks * EXPERT_BLOCK
    slot_tok = jnp.full((n_rows,), t, jnp.int32).at[dest].set(tok_sorted)
    slot_gate = jnp.zeros((n_rows,), jnp.float32).at[dest].set(gate_sorted)
    block_e = jnp.minimum(jnp.searchsorted(pad_end, jnp.arange(n_blocks, dtype=jnp.int32) * EXPERT_BLOCK, side='right'),
                          N_EXPERTS - 1)
    xpad = jnp.concatenate([xf, jnp.zeros((1, d), xf.dtype)], axis=0)

    def run_block(args):
        e, toks, g = args
        xb = xpad[toks]
        hb = jax.nn.silu(xb @ w1_e[e]) * (xb @ w3_e[e])
        return (hb @ w2_e[e]) * g[:, None].astype(xb.dtype)

    yb = lax.map(run_block, (block_e, slot_tok.reshape(n_blocks, EXPERT_BLOCK), slot_gate.reshape(n_blocks, EXPERT_BLOCK)))
    out = jnp.zeros((t + 1, d), xf.dtype).at[slot_tok].add(yb.reshape(n_rows, d))
    return out[:t]


def moe(x, w_router, router_bias, w1_e, w3_e, w2_e, w1_s, w3_s, w2_s):
    n, l, d = x.shape
    xf = x.reshape(n * l, d)
    t = xf.shape[0]
    scores = jax.nn.sigmoid(xf.astype(jnp.float32) @ w_router.astype(jnp.float32))
    biased = scores + router_bias.astype(jnp.float32)
    grp = biased.reshape(t, N_GROUPS, N_EXPERTS // N_GROUPS)
    grp_score = jnp.sum(lax.top_k(grp, 2)[0], axis=-1)
    _, top_g = lax.top_k(grp_score, TOPK_GROUPS)
    gmask = jnp.any(top_g[:, :, None] == jnp.arange(N_GROUPS)[None, None, :], axis=1)
    emask = jnp.repeat(gmask, N_EXPERTS // N_GROUPS, axis=1)
    _, idx = lax.top_k(jnp.where(emask, biased, -jnp.inf), TOP_K)
    w_sel = jnp.take_along_axis(scores, idx, axis=1)
    gates = w_sel / jnp.sum(w_sel, axis=-1, keepdims=True) * ROUTED_SCALE
    y = swiglu(xf, w1_s, w3_s, w2_s) + routed_experts(xf, idx, gates, w1_e, w3_e, w2_e)
    return y.reshape(n, l, d)


def decoder_layer(x, mem_k, mem_v, h0, buf_a, buf_b, p):
    z = x @ p['w_in'] + p['b_in']
    g, u, v, gg = jnp.split(z, [D_REC, 2 * D_REC, 2 * D_REC + D_CONV], axis=-1)
    u_c, new_buf_a = causal_dwconv(u, buf_a, p['w_conv_a'], p['b_conv_a'])
    h, h_last = rglru(u_c, h0, p['w_rg_a'], p['b_rg_a'], p['w_rg_x'], p['b_rg_x'], p['rg_lambda'])
    y_a = jax.nn.gelu(g) * h.astype(x.dtype)
    glu = v * jax.nn.sigmoid(gg)
    c, new_buf_b = causal_dwconv(glu, buf_b, p['w_conv_b'], p['b_conv_b'])
    y_b = jax.nn.silu(layer_norm(c, p['ln_c_g'], p['ln_c_b']))
    mix = jnp.concatenate([y_a, y_b], axis=-1) @ p['w_out']
    x = layer_norm(ALPHA * x + mix, p['ln1_g'], p['ln1_b'])
    x = layer_norm(ALPHA * x + cross_attn(x, mem_k, mem_v, p['w_q'], p['w_o']), p['ln2_g'], p['ln2_b'])
    f = moe(x, p['w_router'], p['router_bias'], p['w1_e'], p['w3_e'], p['w2_e'], p['w1_s'], p['w3_s'], p['w2_s'])
    x = layer_norm(ALPHA * x + f, p['ln3_g'], p['ln3_b'])
    return x, h_last.astype(h0.dtype), new_buf_a, new_buf_b


def setup_inputs(seed: int = 0) -> dict:
    key = jax.random.key(seed)
    ks = iter(jax.random.split(key, 48))

    def nrm(shape, scale):
        return jax.random.normal(next(ks), shape, jnp.float32) * scale

    def gain(shape):
        return 1.0 + nrm(shape, 0.02)

    u_a = jax.random.uniform(next(ks), (DEPTH, D_REC), jnp.float32, minval=0.9, maxval=0.999)
    a_base = u_a ** (1.0 / RG_C)
    rg_lambda = jnp.log(a_base) - jnp.log1p(-a_base)
    return {
        'x_prompt': nrm((BATCH, SEQ, D_MODEL), 1.0),
        'x_sample': nrm((DEC_BATCH, DEC_SEQ, D_MODEL), 1.0),
        'mem_prompt': nrm((BATCH, N_MEM, D_MODEL), 1.0),
        'state_rglru_h': nrm((DEPTH, DEC_BATCH, D_REC), 0.5),
        'state_rglru_conv': nrm((DEPTH, DEC_BATCH, REC_CONV_W - 1, D_REC), 1.0),
        'state_dwconv': nrm((DEPTH, DEC_BATCH, CONF_CONV_W - 1, D_CONV), 0.5),
        'cache_mem_k': nrm((DEPTH, DEC_BATCH, N_MEM, XA_HEADS, XA_HEAD_DIM), 1.0),
        'cache_mem_v': nrm((DEPTH, DEC_BATCH, N_MEM, XA_HEADS, XA_HEAD_DIM), BETA),
        'w_in': nrm((DEPTH, D_MODEL, 2 * D_REC + 2 * D_CONV), D_MODEL ** -0.5),
        'b_in': nrm((DEPTH, 2 * D_REC + 2 * D_CONV), 0.02),
        'w_conv_a': nrm((DEPTH, REC_CONV_W, D_REC), REC_CONV_W ** -0.5),
        'b_conv_a': nrm((DEPTH, D_REC), 0.02),
        'w_rg_a': nrm((DEPTH, REC_HEADS, REC_HEAD_DIM, REC_HEAD_DIM), REC_HEAD_DIM ** -0.5),
        'b_rg_a': nrm((DEPTH, D_REC), 0.02),
        'w_rg_x': nrm((DEPTH, REC_HEADS, REC_HEAD_DIM, REC_HEAD_DIM), REC_HEAD_DIM ** -0.5),
        'b_rg_x': nrm((DEPTH, D_REC), 0.02),
        'rg_lambda': rg_lambda,
        'w_conv_b': nrm((DEPTH, CONF_CONV_W, D_CONV), CONF_CONV_W ** -0.5),
        'b_conv_b': nrm((DEPTH, D_CONV), 0.02),
        'ln_c_g': gain((DEPTH, D_CONV)),
        'ln_c_b': nrm((DEPTH, D_CONV), 0.02),
        'w_out': nrm((DEPTH, D_MIX, D_MODEL), BETA * D_MIX ** -0.5),
        'ln1_g': gain((DEPTH, D_MODEL)),
        'ln1_b': nrm((DEPTH, D_MODEL), 0.02),
        'w_q': nrm((DEPTH, D_MODEL, XA_HEADS * XA_HEAD_DIM), D_MODEL ** -0.5),
        'w_k': nrm((DEPTH, D_MODEL, XA_HEADS * XA_HEAD_DIM), D_MODEL ** -0.5),
        'w_v': nrm((DEPTH, D_MODEL, XA_HEADS * XA_HEAD_DIM), BETA * D_MODEL ** -0.5),
        'w_o': nrm((DEPTH, XA_HEADS * XA_HEAD_DIM, D_MODEL), BETA * (XA_HEADS * XA_HEAD_DIM) ** -0.5),
        'ln2_g': gain((DEPTH, D_MODEL)),
        'ln2_b': nrm((DEPTH, D_MODEL), 0.02),
        'w_router': nrm((DEPTH, D_MODEL, N_EXPERTS), D_MODEL ** -0.5),
        'router_bias': nrm((DEPTH, N_EXPERTS), 0.01),
        'w1_e': nrm((DEPTH, N_EXPERTS, D_MODEL, D_EXPERT), D_MODEL ** -0.5),
        'w3_e': nrm((DEPTH, N_EXPERTS, D_MODEL, D_EXPERT), D_MODEL ** -0.5),
        'w2_e': nrm((DEPTH, N_EXPERTS, D_EXPERT, D_MODEL), BETA * D_EXPERT ** -0.5),
        'w1_s': nrm((DEPTH, D_MODEL, D_SHARED), D_MODEL ** -0.5),
        'w3_s': nrm((DEPTH, D_MODEL, D_SHARED), D_MODEL ** -0.5),
        'w2_s': nrm((DEPTH, D_SHARED, D_MODEL), BETA * D_SHARED ** -0.5),
        'ln3_g': gain((DEPTH, D_MODEL)),
        'ln3_b': nrm((DEPTH, D_MODEL), 0.02),
    }


def reference(x_prompt, x_sample, mem_prompt, state_rglru_h, state_rglru_conv, state_dwconv, cache_mem_k, cache_mem_v,
              w_in, b_in, w_conv_a, b_conv_a, w_rg_a, b_rg_a, w_rg_x, b_rg_x, rg_lambda, w_conv_b, b_conv_b,
              ln_c_g, ln_c_b, w_out, ln1_g, ln1_b, w_q, w_k, w_v, w_o, ln2_g, ln2_b, w_router, router_bias,
              w1_e, w3_e, w2_e, w1_s, w3_s, w2_s, ln3_g, ln3_b):
    n_p = x_prompt.shape[0]
    n_mem = mem_prompt.shape[1]
    yp, ys = x_prompt, x_sample
    hp_l, cap_l, cbp_l, mk_l, mv_l = [], [], [], [], []
    hs_l, cas_l, cbs_l = [], [], []
    for l in range(DEPTH):
        p = dict(w_in=w_in[l], b_in=b_in[l], w_conv_a=w_conv_a[l], b_conv_a=b_conv_a[l], w_rg_a=w_rg_a[l],
                 b_rg_a=b_rg_a[l], w_rg_x=w_rg_x[l], b_rg_x=b_rg_x[l], rg_lambda=rg_lambda[l],
                 w_conv_b=w_conv_b[l], b_conv_b=b_conv_b[l], ln_c_g=ln_c_g[l], ln_c_b=ln_c_b[l],
                 w_out=w_out[l], ln1_g=ln1_g[l], ln1_b=ln1_b[l], w_q=w_q[l], w_o=w_o[l],
                 ln2_g=ln2_g[l], ln2_b=ln2_b[l], w_router=w_router[l], router_bias=router_bias[l],
                 w1_e=w1_e[l], w3_e=w3_e[l], w2_e=w2_e[l], w1_s=w1_s[l], w3_s=w3_s[l], w2_s=w2_s[l],
                 ln3_g=ln3_g[l], ln3_b=ln3_b[l])
        mk_p = (mem_prompt @ w_k[l]).reshape(n_p, n_mem, XA_HEADS, XA_HEAD_DIM)
        mv_p = (mem_prompt @ w_v[l]).reshape(n_p, n_mem, XA_HEADS, XA_HEAD_DIM)
        h0_p = jnp.zeros((n_p, D_REC), x_prompt.dtype)
        bufa_p = jnp.zeros((n_p, REC_CONV_W - 1, D_REC), x_prompt.dtype)
        bufb_p = jnp.zeros((n_p, CONF_CONV_W - 1, D_CONV), x_prompt.dtype)
        yp, hp, cap, cbp = decoder_layer(yp, mk_p, mv_p, h0_p, bufa_p, bufb_p, p)
        ys, hs, cas, cbs = decoder_layer(ys, cache_mem_k[l], cache_mem_v[l], state_rglru_h[l],
                                         state_rglru_conv[l], state_dwconv[l], p)
        hp_l.append(hp); cap_l.append(cap); cbp_l.append(cbp); mk_l.append(mk_p); mv_l.append(mv_p)
        hs_l.append(hs); cas_l.append(cas); cbs_l.append(cbs)
    return (yp, ys, jnp.stack(hp_l), jnp.stack(cap_l), jnp.stack(cbp_l), jnp.stack(mk_l), jnp.stack(mv_l),
            jnp.stack(hs_l), jnp.stack(cas_l), jnp.stack(cbs_l))
```

```python
import functools

import jax
import jax.numpy as jnp
from jax import lax
from jax.experimental import pallas as pl
from jax.experimental.pallas import tpu as pltpu

F32 = jnp.float32
BF16 = jnp.bfloat16

D_MODEL = 1024
D_REC = 512
D_CONV = 512
REC_HEADS = 8
REC_HEAD_DIM = D_REC // REC_HEADS
REC_CONV_W = 4
RG_C = 8.0
CONF_CONV_W = 31
XA_HEADS = 4
XA_HEAD_DIM = D_MODEL // XA_HEADS
N_EXPERTS = 64
TOP_K = 8
N_GROUPS = 8
TOPK_GROUPS = 4
GROUP_SIZE = N_EXPERTS // N_GROUPS
D_EXPERT = D_MODEL // 4
ROUTED_SCALE = 2.5
LN_EPS = 1e-5
DEPTH = 1
ALPHA = (2 * DEPTH) ** 0.25

LANES = 128
SUBLANES = 8
VMEM_LIMIT = 56 * 1024 * 1024

MIX_TL = 64
SCAN_PAD = 8
ATT_TM = 512
ATT_G = 4
TOK_TM = 384
MOE_NSEG = 8
MOE_R = 128
ROW_SLAB = D_MODEL // LANES
TILE_S = MOE_R + 8


def _cp(sem=None):
    return pltpu.CompilerParams(dimension_semantics=sem, vmem_limit_bytes=VMEM_LIMIT)


def _ln(x, g, b):
    mu = jnp.mean(x, axis=-1, keepdims=True)
    xc = x - mu
    var = jnp.mean(xc * xc, axis=-1, keepdims=True)
    return xc * lax.rsqrt(var + LN_EPS) * g + b


def _sigmoid(x):
    return 1.0 / (1.0 + jnp.exp(-x))


def _silu(x):
    return x * _sigmoid(x)


def _softplus(x):
    return jnp.maximum(x, 0.0) + jnp.log1p(jnp.exp(-jnp.abs(x)))


def _gelu_tanh(x):
    c = 0.7978845608028654
    return 0.5 * x * (1.0 + jnp.tanh(c * (x + 0.044715 * (x * x * x))))


def _dot(a, b):
    return jnp.dot(a, b, preferred_element_type=F32)


def _full(shape):
    n = len(shape)
    return pl.BlockSpec(shape, lambda *_: (0,) * n)


def _rglru_coeffs(ucf, wg_ref, bg_ref, lam_ref):
    gates = _dot(ucf.astype(BF16), wg_ref[...]) + bg_ref[...]
    r = _sigmoid(gates[:, :D_REC])
    ig = _sigmoid(gates[:, D_REC:])
    log_a = (-RG_C) * r * _softplus(-lam_ref[...])
    a = jnp.exp(log_a)
    bt = jnp.sqrt(-jnp.tanh(log_a) * (a * a + 1.0)) * (ig * ucf)
    return a, bt


def _mix_prompt_kernel(x_ref, win_ref, bin_ref, wca_ref, bca_ref, wg_ref, bg_ref, lam_ref,
                       wcb_ref, bcb_ref, lcg_ref, lcb_ref, wout_ref, l1g_ref, l1b_ref,
                       x1_ref, hl_ref, ca_ref, cb_ref,
                       exta, extb, a_s, b_s, h_s, hc, *, nb, tl):
    i = pl.program_id(0)
    m = nb * tl
    tlp = tl + SCAN_PAD
    pa = SUBLANES
    pb = 4 * SUBLANES
    wa = REC_CONV_W
    wb = CONF_CONV_W

    @pl.when(i == 0)
    def _():
        exta[:, 0:pa, :] = jnp.zeros((nb, pa, D_REC), F32)
        extb[:, 0:pb, :] = jnp.zeros((nb, pb, D_CONV), F32)
        hc[...] = jnp.zeros((nb, D_REC), F32)

    xf = x_ref[...].reshape(m, D_MODEL)
    z = _dot(xf.astype(BF16), win_ref[...]) + bin_ref[...]
    g = z[:, 0:D_REC]
    u = z[:, D_REC:2 * D_REC]
    v = z[:, 2 * D_REC:2 * D_REC + D_CONV]
    gg = z[:, 2 * D_REC + D_CONV:]

    exta[:, pa:pa + tl, :] = u.reshape(nb, tl, D_REC)
    uc = jnp.zeros((nb, tl, D_REC), F32) + bca_ref[...]
    for j in range(wa):
        s0 = pa - (wa - 1) + j
        uc = uc + wca_ref[j:j + 1, :] * exta[:, s0:s0 + tl, :]
    tail_a = exta[:, pa + tl - (wa - 1):pa + tl, :]
    ca_ref[...] = tail_a
    exta[:, pa - (wa - 1):pa, :] = tail_a

    ucf = uc.reshape(m, D_REC)
    a, bt = _rglru_coeffs(ucf, wg_ref, bg_ref, lam_ref)

    nlt = D_REC // LANES
    for n in range(nb):
        for c in range(nlt):
            a_s[c, pl.ds(n * tlp, tl), :] = a[n * tl:(n + 1) * tl, c * LANES:(c + 1) * LANES]
            b_s[c, pl.ds(n * tlp, tl), :] = bt[n * tl:(n + 1) * tl, c * LANES:(c + 1) * LANES]

    def step(t, hs):
        out = []
        for c in range(nlt):
            hn = a_s[c, pl.ds(t, nb, stride=tlp), :] * hs[c] + b_s[c, pl.ds(t, nb, stride=tlp), :]
            h_s[c, pl.ds(t, nb, stride=tlp), :] = hn
            out.append(hn)
        return tuple(out)

    hs = lax.fori_loop(0, tl, step, tuple(hc[:, c * LANES:(c + 1) * LANES] for c in range(nlt)))
    h = jnp.concatenate(hs, axis=-1)
    hc[...] = h
    hl_ref[...] = h
    hseq = jnp.concatenate(
        [jnp.concatenate([h_s[c, pl.ds(n * tlp, tl), :] for c in range(nlt)], axis=-1) for n in range(nb)],
        axis=0)
    y_a = _gelu_tanh(g) * hseq

    glu = v * _sigmoid(gg)
    extb[:, pb:pb + tl, :] = glu.reshape(nb, tl, D_CONV)
    c = jnp.zeros((nb, tl, D_CONV), F32) + bcb_ref[...]
    for j in range(wb):
        s0 = pb - (wb - 1) + j
        c = c + wcb_ref[j:j + 1, :] * extb[:, s0:s0 + tl, :]
    tail_b = extb[:, pb + tl - (wb - 1):pb + tl, :]
    cb_ref[...] = tail_b
    extb[:, pb - (wb - 1):pb, :] = tail_b
    y_b = _silu(_ln(c.reshape(m, D_CONV), lcg_ref[...], lcb_ref[...]))

    mix = jnp.concatenate([y_a, y_b], axis=-1).astype(BF16)
    o = _dot(mix, wout_ref[...])
    x1 = _ln(ALPHA * xf + o, l1g_ref[...], l1b_ref[...])
    x1_ref[...] = x1.reshape(nb, tl, D_MODEL)


def _mix_prompt(x, wp):
    nb, L, _ = x.shape
    tl = MIX_TL
    tlp = tl + SCAN_PAD
    kern = functools.partial(_mix_prompt_kernel, nb=nb, tl=tl)
    ins = [wp['w_in'], wp['b_in'], wp['w_conv_a'], wp['b_conv_a'], wp['w_gate'], wp['b_gate'],
           wp['lam'], wp['w_conv_b'], wp['b_conv_b'], wp['ln_c_g'], wp['ln_c_b'], wp['w_out'],
           wp['ln1_g'], wp['ln1_b']]
    return pl.pallas_call(
        kern,
        grid=(L // tl,),
        in_specs=[pl.BlockSpec((nb, tl, D_MODEL), lambda i: (0, i, 0))] + [_full(a.shape) for a in ins],
        out_specs=[pl.BlockSpec((nb, tl, D_MODEL), lambda i: (0, i, 0)),
                   _full((nb, D_REC)), _full((nb, REC_CONV_W - 1, D_REC)),
                   _full((nb, CONF_CONV_W - 1, D_CONV))],
        out_shape=[jax.ShapeDtypeStruct((nb, L, D_MODEL), F32),
                   jax.ShapeDtypeStruct((nb, D_REC), F32),
                   jax.ShapeDtypeStruct((nb, REC_CONV_W - 1, D_REC), F32),
                   jax.ShapeDtypeStruct((nb, CONF_CONV_W - 1, D_CONV), F32)],
        scratch_shapes=[pltpu.VMEM((nb, SUBLANES + tl, D_REC), F32),
                        pltpu.VMEM((nb, 4 * SUBLANES + tl, D_CONV), F32),
                        pltpu.VMEM((D_REC // LANES, nb * tlp, LANES), F32),
                        pltpu.VMEM((D_REC // LANES, nb * tlp, LANES), F32),
                        pltpu.VMEM((D_REC // LANES, nb * tlp, LANES), F32),
                        pltpu.VMEM((nb, D_REC), F32)],
        compiler_params=_cp(("arbitrary",)),
        name="mix_prompt",
    )(x, *ins)


def _mix_sample_kernel(x_ref, sta_ref, stb_ref, h0_ref, win_ref, bin_ref, wca_ref, bca_ref, wg_ref,
                       bg_ref, lam_ref, wcb_ref, bcb_ref, lcg_ref, lcb_ref, wout_ref, l1g_ref, l1b_ref,
                       x1_ref, h_ref, u_ref, glu_ref):
    wa = REC_CONV_W
    wb = CONF_CONV_W
    xf = x_ref[...]
    z = _dot(xf.astype(BF16), win_ref[...]) + bin_ref[...]
    g = z[:, 0:D_REC]
    u = z[:, D_REC:2 * D_REC]
    v = z[:, 2 * D_REC:2 * D_REC + D_CONV]
    gg = z[:, 2 * D_REC + D_CONV:]

    uc = bca_ref[...] + wca_ref[wa - 1:wa, :] * u
    for j in range(wa - 1):
        uc = uc + wca_ref[j:j + 1, :] * sta_ref[j]
    a, bt = _rglru_coeffs(uc, wg_ref, bg_ref, lam_ref)
    h = a * h0_ref[...] + bt
    y_a = _gelu_tanh(g) * h

    glu = v * _sigmoid(gg)
    c = bcb_ref[...] + wcb_ref[wb - 1:wb, :] * glu
    for j in range(wb - 1):
        c = c + wcb_ref[j:j + 1, :] * stb_ref[j]
    y_b = _silu(_ln(c, lcg_ref[...], lcb_ref[...]))

    mix = jnp.concatenate([y_a, y_b], axis=-1).astype(BF16)
    o = _dot(mix, wout_ref[...])
    x1_ref[...] = _ln(ALPHA * xf + o, l1g_ref[...], l1b_ref[...])
    h_ref[...] = h
    u_ref[...] = u
    glu_ref[...] = glu


def _mix_sample(x, sta, stb, h0, wp):
    n = x.shape[0]
    ins = [x, sta, stb, h0, wp['w_in'], wp['b_in'], wp['w_conv_a'], wp['b_conv_a'], wp['w_gate'],
           wp['b_gate'], wp['lam'], wp['w_conv_b'], wp['b_conv_b'], wp['ln_c_g'], wp['ln_c_b'],
           wp['w_out'], wp['ln1_g'], wp['ln1_b']]
    return pl.pallas_call(
        _mix_sample_kernel,
        grid=(1,),
        in_specs=[_full(a.shape) for a in ins],
        out_specs=[_full((n, D_MODEL)), _full((n, D_REC)), _full((n, D_REC)), _full((n, D_CONV))],
        out_shape=[jax.ShapeDtypeStruct((n, D_MODEL), F32), jax.ShapeDtypeStruct((n, D_REC), F32),
                   jax.ShapeDtypeStruct((n, D_REC), F32), jax.ShapeDtypeStruct((n, D_CONV), F32)],
        compiler_params=_cp(("arbitrary",)),
        name="mix_sample",
    )(*ins)


def _linear_kernel(a_ref, w_ref, o_ref):
    o_ref[...] = _dot(a_ref[...].astype(BF16), w_ref[...])


def _linear(a, w, tm, name):
    m, k = a.shape
    n = w.shape[1]
    return pl.pallas_call(
        _linear_kernel,
        grid=(m // tm,),
        in_specs=[pl.BlockSpec((tm, k), lambda i: (i, 0)), _full(w.shape)],
        out_specs=pl.BlockSpec((tm, n), lambda i: (i, 0)),
        out_shape=jax.ShapeDtypeStruct((m, n), F32),
        compiler_params=_cp(("parallel",)),
        name=name,
    )(a, w)


def _linear_res_ln_kernel(a_ref, r_ref, w_ref, g_ref, b_ref, o_ref):
    o = _dot(a_ref[...].astype(BF16), w_ref[...])
    o_ref[...] = _ln(ALPHA * r_ref[...] + o, g_ref[...], b_ref[...])


def _linear_res_ln(a, res, w, g, b, tm, name):
    m, k = a.shape
    n = w.shape[1]
    return pl.pallas_call(
        _linear_res_ln_kernel,
        grid=(m // tm,),
        in_specs=[pl.BlockSpec((tm, k), lambda i: (i, 0)), pl.BlockSpec((tm, n), lambda i: (i, 0)),
                  _full(w.shape), _full(g.shape), _full(b.shape)],
        out_specs=pl.BlockSpec((tm, n), lambda i: (i, 0)),
        out_shape=jax.ShapeDtypeStruct((m, n), F32),
        compiler_params=_cp(("parallel",)),
        name=name,
    )(a, res, w, g, b)


def _attn_prompt_kernel(x_ref, k_ref, v_ref, wq_ref, wo_ref, g_ref, b_ref, o_ref):
    x = x_ref[0]
    q = _dot(x.astype(BF16), wq_ref[...])
    k = k_ref[0].astype(BF16)
    v = v_ref[0].astype(BF16)
    scale = XA_HEAD_DIM ** -0.5
    outs = []
    for h in range(XA_HEADS):
        sl = slice(h * XA_HEAD_DIM, (h + 1) * XA_HEAD_DIM)
        s = lax.dot_general(q[:, sl].astype(BF16), k[:, sl], (((1,), (1,)), ((), ())),
                            preferred_element_type=F32) * scale
        e = jnp.exp(s - jnp.max(s, axis=-1, keepdims=True))
        p = e / jnp.sum(e, axis=-1, keepdims=True)
        outs.append(_dot(p.astype(BF16), v[:, sl]))
    o = jnp.concatenate(outs, axis=-1).astype(BF16)
    att = _dot(o, wo_ref[...])
    o_ref[0] = _ln(ALPHA * x + att, g_ref[...], b_ref[...])


def _attn_prompt(x1, mk, mv, wq, wo, g, b):
    nb, L, d = x1.shape
    nm = mk.shape[1]
    tm = ATT_TM
    return pl.pallas_call(
        _attn_prompt_kernel,
        grid=(nb, L // tm),
        in_specs=[pl.BlockSpec((1, tm, d), lambda bi, i: (bi, i, 0)),
                  pl.BlockSpec((1, nm, d), lambda bi, i: (bi, 0, 0)),
                  pl.BlockSpec((1, nm, d), lambda bi, i: (bi, 0, 0)),
                  _full(wq.shape), _full(wo.shape), _full(g.shape), _full(b.shape)],
        out_specs=pl.BlockSpec((1, tm, d), lambda bi, i: (bi, i, 0)),
        out_shape=jax.ShapeDtypeStruct((nb, L, d), F32),
        compiler_params=_cp(("parallel", "parallel")),
        name="attn_prompt",
    )(x1, mk, mv, wq, wo, g, b)


def _attn_sample_kernel(q_ref, k_ref, v_ref, o_ref):
    scale = XA_HEAD_DIM ** -0.5
    for gi in range(ATT_G):
        qn = q_ref[gi]
        prod = k_ref[gi] * qn
        vn = v_ref[gi]
        cols = []
        for h in range(XA_HEADS):
            sl = slice(h * XA_HEAD_DIM, (h + 1) * XA_HEAD_DIM)
            s = jnp.sum(prod[:, sl], axis=-1, keepdims=True) * scale
            e = jnp.exp(s - jnp.max(s, axis=0, keepdims=True))
            p = e / jnp.sum(e, axis=0, keepdims=True)
            cols.append(jnp.sum(p * vn[:, sl], axis=0, keepdims=True))
        o_ref[gi] = jnp.concatenate(cols, axis=-1)


def _attn_sample(q, ck, cv):
    n, nm, d = ck.shape
    q3 = q.reshape(n, 1, d)
    out = pl.pallas_call(
        _attn_sample_kernel,
        grid=(n // ATT_G,),
        in_specs=[pl.BlockSpec((ATT_G, 1, d), lambda i: (i, 0, 0)),
                  pl.BlockSpec((ATT_G, nm, d), lambda i: (i, 0, 0)),
                  pl.BlockSpec((ATT_G, nm, d), lambda i: (i, 0, 0))],
        out_specs=pl.BlockSpec((ATT_G, 1, d), lambda i: (i, 0, 0)),
        out_shape=jax.ShapeDtypeStruct((n, 1, d), F32),
        compiler_params=_cp(("parallel",)),
        name="attn_sample",
    )(q3, ck, cv)
    return out.reshape(n, d)


def _first_argmax(val, iota, n, axis):
    mx = jnp.max(val, axis=axis, keepdims=True)
    idx = jnp.min(jnp.where(val == mx, iota, n), axis=axis, keepdims=True)
    return mx, idx


def _router_kernel(x_ref, wr_ref, rb_ref, idx_ref, gate_ref):
    tm = x_ref.shape[0]
    neg = -jnp.inf
    logits = lax.dot_general(wr_ref[...], x_ref[...], (((1,), (1,)), ((), ())),
                             precision=lax.Precision.HIGHEST, preferred_element_type=F32)
    scores = _sigmoid(logits)
    biased = scores + rb_ref[...]
    grp = biased.reshape(N_GROUPS, GROUP_SIZE, tm)
    io_g = lax.broadcasted_iota(jnp.int32, (N_GROUPS, GROUP_SIZE, tm), 1)
    m1, i1 = _first_argmax(grp, io_g, GROUP_SIZE, 1)
    m2 = jnp.max(jnp.where(io_g == i1, neg, grp), axis=1, keepdims=True)
    gscore = (m1 + m2).reshape(N_GROUPS, tm)
    io_n = lax.broadcasted_iota(jnp.int32, (N_GROUPS, tm), 0)
    gsel = jnp.zeros((N_GROUPS, tm), jnp.bool_)
    for _ in range(TOPK_GROUPS):
        _, gi = _first_argmax(gscore, io_n, N_GROUPS, 0)
        hit = io_n == gi
        gsel = jnp.logical_or(gsel, hit)
        gscore = jnp.where(hit, neg, gscore)
    emask = jnp.broadcast_to(gsel.reshape(N_GROUPS, 1, tm), (N_GROUPS, GROUP_SIZE, tm)).reshape(N_EXPERTS, tm)
    val = jnp.where(emask, biased, neg)
    io_e = lax.broadcasted_iota(jnp.int32, (N_EXPERTS, tm), 0)
    idxs, ws = [], []
    for _ in range(TOP_K):
        _, ei = _first_argmax(val, io_e, N_EXPERTS, 0)
        hit = io_e == ei
        ws.append(jnp.sum(jnp.where(hit, scores, 0.0), axis=0, keepdims=True))
        idxs.append(ei)
        val = jnp.where(hit, neg, val)
    w = jnp.concatenate(ws, axis=0)
    idx_ref[...] = jnp.concatenate(idxs, axis=0)
    gate_ref[...] = w / jnp.sum(w, axis=0, keepdims=True) * ROUTED_SCALE


def _router(x2, wr_t, rb):
    t, d = x2.shape
    tm = TOK_TM
    return pl.pallas_call(
        _router_kernel,
        grid=(t // tm,),
        in_specs=[pl.BlockSpec((tm, d), lambda i: (i, 0)), _full(wr_t.shape), _full(rb.shape)],
        out_specs=[pl.BlockSpec((TOP_K, tm), lambda i: (0, i)), pl.BlockSpec((TOP_K, tm), lambda i: (0, i))],
        out_shape=[jax.ShapeDtypeStruct((TOP_K, t), jnp.int32), jax.ShapeDtypeStruct((TOP_K, t), F32)],
        compiler_params=_cp(("parallel",)),
        name="router",
    )(x2, wr_t, rb)


def _expert_kernel(be_ref, bn_ref, slot_ref, x_ref, gate_ref, w13_ref, w2_ref, acc_ref, tile, ys):
    del be_ref
    b = pl.program_id(0)
    r = MOE_R
    s = TILE_S

    @pl.when(b == 0)
    def _():
        acc_ref[...] = jnp.zeros(acc_ref.shape, F32)
        tile[...] = jnp.zeros(tile.shape, F32)

    n = bn_ref[b]

    @pl.when(n > 0)
    def _():
        base = b * r

        def gather(i, carry):
            off = pl.multiple_of(slot_ref[base + i], ROW_SLAB)
            tile[pl.ds(i, ROW_SLAB, stride=s), :] = x_ref[pl.ds(off, ROW_SLAB), :]
            return carry

        lax.fori_loop(0, n, gather, 0)
        xb = jnp.concatenate([tile[c * s:c * s + r, :] for c in range(ROW_SLAB)], axis=-1).astype(BF16)
        h13 = _dot(xb, w13_ref[0])
        hh = (_silu(h13[:, :D_EXPERT]) * h13[:, D_EXPERT:]).astype(BF16)
        y = _dot(hh, w2_ref[0]) * gate_ref[...]
        for g in range(r // SUBLANES):
            for c in range(ROW_SLAB):
                ys[pl.ds((g * ROW_SLAB + c) * SUBLANES, SUBLANES), :] = (
                    y[g * SUBLANES:(g + 1) * SUBLANES, c * LANES:(c + 1) * LANES])

        def scatter(q, carry):
            rows = [q * 4 + j for j in range(4)]
            offs = [pl.multiple_of(slot_ref[base + ri], ROW_SLAB) for ri in rows]
            vals = []
            for ri, off in zip(rows, offs):
                src = (ri // SUBLANES) * (ROW_SLAB * SUBLANES) + ri % SUBLANES
                vals.append(acc_ref[pl.ds(off, ROW_SLAB), :] + ys[pl.ds(src, ROW_SLAB, stride=SUBLANES), :])
            for off, val in zip(offs, vals):
                acc_ref[pl.ds(off, ROW_SLAB), :] = val
            return carry

        lax.fori_loop(0, (n + 3) // 4, scatter, 0)


def _expert_segment(blk_e, blk_n, slot_off, xslab, gate_col, w13, w2, tseg):
    nblk = blk_e.shape[0]
    r = MOE_R
    acc_rows = (tseg + SUBLANES) * ROW_SLAB
    grid_spec = pltpu.PrefetchScalarGridSpec(
        num_scalar_prefetch=3,
        grid=(nblk,),
        in_specs=[pl.BlockSpec(xslab.shape, lambda b, be, bn, sl: (0, 0)),
                  pl.BlockSpec((r, 1), lambda b, be, bn, sl: (b, 0)),
                  pl.BlockSpec((1, D_MODEL, 2 * D_EXPERT), lambda b, be, bn, sl: (be[b], 0, 0)),
                  pl.BlockSpec((1, D_EXPERT, D_MODEL), lambda b, be, bn, sl: (be[b], 0, 0))],
        out_specs=pl.BlockSpec((acc_rows, LANES), lambda b, be, bn, sl: (0, 0)),
        scratch_shapes=[pltpu.VMEM((ROW_SLAB * TILE_S, LANES), F32),
                        pltpu.VMEM((r * ROW_SLAB, LANES), F32)],
    )
    return pl.pallas_call(
        _expert_kernel,
        grid_spec=grid_spec,
        out_shape=jax.ShapeDtypeStruct((acc_rows, LANES), F32),
        compiler_params=_cp(("arbitrary",)),
        name="experts",
    )(blk_e, blk_n, slot_off, xslab, gate_col, w13, w2)


def _routing_plan(idx_t, gate_t, nseg, tseg, nblk):
    r = MOE_R
    idx = idx_t.T.reshape(nseg, tseg, TOP_K)
    gates = gate_t.T.reshape(nseg, tseg * TOP_K)
    onehot = jnp.sum((idx[..., None] == jnp.arange(N_EXPERTS, dtype=jnp.int32)).astype(jnp.int32), axis=2)
    csum = jnp.cumsum(onehot, axis=1)
    rank = csum - onehot
    counts = csum[:, -1, :]
    padded = (counts + r - 1) // r * r
    pad_end = jnp.cumsum(padded, axis=1)
    pad_start = pad_end - padded
    dest = jnp.take_along_axis(pad_start[:, None, :] + rank, idx, axis=2).reshape(nseg, tseg * TOP_K)
    seg_ids = jnp.broadcast_to(jnp.arange(nseg, dtype=jnp.int32)[:, None], dest.shape)
    assign = jnp.broadcast_to(jnp.arange(tseg * TOP_K, dtype=jnp.int32)[None, :], dest.shape)
    slot_assign = jnp.full((nseg, nblk * r), -1, jnp.int32).at[seg_ids, dest].set(
        assign, unique_indices=True, mode='drop')
    valid = slot_assign >= 0
    safe = jnp.maximum(slot_assign, 0)
    slot_off = jnp.where(valid, (safe // TOP_K) * ROW_SLAB, tseg * ROW_SLAB).astype(jnp.int32)
    slot_gate = jnp.where(valid, jnp.take_along_axis(gates, safe, axis=1), 0.0)
    blk_pos = jnp.arange(nblk, dtype=jnp.int32) * r
    blk_e = jnp.minimum(jnp.sum((blk_pos[None, :, None] >= pad_end[:, None, :]).astype(jnp.int32), axis=2),
                        N_EXPERTS - 1)
    cnt_e = jnp.take_along_axis(counts, blk_e, axis=1)
    start_e = jnp.take_along_axis(pad_start, blk_e, axis=1)
    blk_n = jnp.clip(cnt_e - (blk_pos[None, :] - start_e), 0, r).astype(jnp.int32)
    return blk_e.astype(jnp.int32), blk_n, slot_off, slot_gate


def _final_kernel(x_ref, r_ref, w13_ref, w2_ref, g_ref, b_ref, o_ref):
    x = x_ref[...]
    h13 = _dot(x.astype(BF16), w13_ref[...])
    hh = (_silu(h13[:, :D_EXPERT]) * h13[:, D_EXPERT:]).astype(BF16)
    sh = _dot(hh, w2_ref[...])
    o_ref[...] = _ln(ALPHA * x + (sh + r_ref[...]), g_ref[...], b_ref[...])


def _final(x2, routed, w13, w2, g, b):
    t, d = x2.shape
    tm = TOK_TM
    return pl.pallas_call(
        _final_kernel,
        grid=(t // tm,),
        in_specs=[pl.BlockSpec((tm, d), lambda i: (i, 0)), pl.BlockSpec((tm, d), lambda i: (i, 0)),
                  _full(w13.shape), _full(w2.shape), _full(g.shape), _full(b.shape)],
        out_specs=pl.BlockSpec((tm, d), lambda i: (i, 0)),
        out_shape=jax.ShapeDtypeStruct((t, d), F32),
        compiler_params=_cp(("parallel",)),
        name="final",
    )(x2, routed, w13, w2, g, b)


def _block_diag_dense(w):
    h, dh, _ = w.shape
    eye = jnp.eye(h, dtype=w.dtype)
    return jnp.einsum('hij,hg->higj', w, eye).reshape(h * dh, h * dh)


def kernel(x_prompt, x_sample, mem_prompt, state_rglru_h, state_rglru_conv, state_dwconv, cache_mem_k, cache_mem_v, w_in, b_in, w_conv_a, b_conv_a, w_rg_a, b_rg_a, w_rg_x, b_rg_x, rg_lambda, w_conv_b, b_conv_b, ln_c_g, ln_c_b, w_out, ln1_g, ln1_b, w_q, w_k, w_v, w_o, ln2_g, ln2_b, w_router, router_bias, w1_e, w3_e, w2_e, w1_s, w3_s, w2_s, ln3_g, ln3_b):
    assert w_in.shape[0] == 1, "single-layer trunk"
    nb, L, d = x_prompt.shape
    ns = x_sample.shape[0]
    nm = mem_prompt.shape[1]
    row = lambda a: a.reshape(1, -1)

    wp = dict(
        w_in=w_in[0].astype(BF16), b_in=row(b_in[0]),
        w_conv_a=w_conv_a[0], b_conv_a=row(b_conv_a[0]),
        w_gate=jnp.concatenate([_block_diag_dense(w_rg_a[0]), _block_diag_dense(w_rg_x[0])], axis=1).astype(BF16),
        b_gate=row(jnp.concatenate([b_rg_a[0], b_rg_x[0]])),
        lam=row(rg_lambda[0]),
        w_conv_b=w_conv_b[0], b_conv_b=row(b_conv_b[0]),
        ln_c_g=row(ln_c_g[0]), ln_c_b=row(ln_c_b[0]),
        w_out=w_out[0].astype(BF16), ln1_g=row(ln1_g[0]), ln1_b=row(ln1_b[0]),
    )
    wq = w_q[0].astype(BF16)
    wo = w_o[0].astype(BF16)
    wkv = jnp.concatenate([w_k[0], w_v[0]], axis=1).astype(BF16)
    l2g, l2b = row(ln2_g[0]), row(ln2_b[0])

    x1_p, h_p, ca_p, cb_p = _mix_prompt(x_prompt, wp)
    kv = _linear(mem_prompt.reshape(nb * nm, d), wkv, 512, "kv_proj")
    mk_p = kv[:, :d].reshape(nb, nm, d)
    mv_p = kv[:, d:].reshape(nb, nm, d)
    x2_p = _attn_prompt(x1_p, mk_p, mv_p, wq, wo, l2g, l2b)

    xs = x_sample.reshape(ns, d)
    sta = jnp.transpose(state_rglru_conv[0], (1, 0, 2))
    stb = jnp.transpose(state_dwconv[0], (1, 0, 2))
    x1_s, h_s, u_s, glu_s = _mix_sample(xs, sta, stb, state_rglru_h[0], wp)
    q_s = _linear(x1_s, wq, ns, "q_sample")
    o_s = _attn_sample(q_s, cache_mem_k[0].reshape(ns, nm, d), cache_mem_v[0].reshape(ns, nm, d))
    x2_s = _linear_res_ln(o_s, x1_s, wo, l2g, l2b, ns, "attn_out_sample")

    x2 = jnp.concatenate([x2_p.reshape(nb * L, d), x2_s], axis=0)
    t = x2.shape[0]
    nseg = MOE_NSEG
    tseg = t // nseg
    assert tseg * nseg == t and tseg % SUBLANES == 0 and t % TOK_TM == 0
    nblk = -(-(tseg * TOP_K + N_EXPERTS * (MOE_R - 1)) // MOE_R)
    idx_t, gate_t = _router(x2, jnp.transpose(w_router[0]), router_bias[0].reshape(N_EXPERTS, 1))
    blk_e, blk_n, slot_off, slot_gate = _routing_plan(idx_t, gate_t, nseg, tseg, nblk)
    w13_e = jnp.concatenate([w1_e[0], w3_e[0]], axis=-1).astype(BF16)
    w2_eb = w2_e[0].astype(BF16)
    xslab = x2.reshape(nseg, tseg * ROW_SLAB, LANES)
    parts = []
    for si in range(nseg):
        acc = _expert_segment(blk_e[si], blk_n[si], slot_off[si], xslab[si],
                              slot_gate[si].reshape(nblk * MOE_R, 1), w13_e, w2_eb, tseg)
        parts.append(acc[:tseg * ROW_SLAB].reshape(tseg, d))
    routed = jnp.concatenate(parts, axis=0)
    w13_s = jnp.concatenate([w1_s[0], w3_s[0]], axis=-1).astype(BF16)
    y = _final(x2, routed, w13_s, w2_s[0].astype(BF16), row(ln3_g[0]), row(ln3_b[0]))

    y_p = y[:nb * L].reshape(nb, L, d)
    y_s = y[nb * L:].reshape(ns, 1, d)
    new_ca_s = jnp.concatenate([state_rglru_conv[0][:, 1:], u_s[:, None, :]], axis=1)
    new_cb_s = jnp.concatenate([state_dwconv[0][:, 1:], glu_s[:, None, :]], axis=1)
    hd = XA_HEAD_DIM
    return (y_p, y_s, h_p[None], ca_p[None], cb_p[None],
            mk_p.reshape(1, nb, nm, XA_HEADS, hd), mv_p.reshape(1, nb, nm, XA_HEADS, hd),
            h_s[None], new_ca_s[None], new_cb_s[None])
```

```python
import functools

import jax
import jax.numpy as jnp
from jax import lax
from jax.experimental import pallas as pl
from jax.experimental.pallas import tpu as pltpu

F32 = jnp.float32
BF16 = jnp.bfloat16

D_MODEL = 1024
D_REC = 512
D_CONV = 512
REC_HEADS = 8
REC_HEAD_DIM = D_REC // REC_HEADS
REC_CONV_W = 4
RG_C = 8.0
CONF_CONV_W = 31
XA_HEADS = 4
XA_HEAD_DIM = D_MODEL // XA_HEADS
N_EXPERTS = 64
TOP_K = 8
N_GROUPS = 8
TOPK_GROUPS = 4
GROUP_SIZE = N_EXPERTS // N_GROUPS
D_EXPERT = D_MODEL // 4
ROUTED_SCALE = 2.5
LN_EPS = 1e-5
DEPTH = 1
ALPHA = (2 * DEPTH) ** 0.25

LANES = 128
SUBLANES = 8
VMEM_LIMIT = 56 * 1024 * 1024

MIX_TL = 64
SCAN_PAD = 8
ATT_TM = 512
ATT_G = 4
TOK_TM = 384
MOE_NSEG = 3
MOE_R = 128
ROW_SLAB = D_MODEL // LANES
TILE_S = MOE_R + 8


def _cp(sem=None):
    return pltpu.CompilerParams(dimension_semantics=sem, vmem_limit_bytes=VMEM_LIMIT)


def _ln(x, g, b):
    mu = jnp.mean(x, axis=-1, keepdims=True)
    xc = x - mu
    var = jnp.mean(xc * xc, axis=-1, keepdims=True)
    return xc * lax.rsqrt(var + LN_EPS) * g + b


def _sigmoid(x):
    return 1.0 / (1.0 + jnp.exp(-x))


def _silu(x):
    return x * _sigmoid(x)


def _softplus(x):
    return jnp.maximum(x, 0.0) + jnp.log1p(jnp.exp(-jnp.abs(x)))


def _gelu_tanh(x):
    c = 0.7978845608028654
    return 0.5 * x * (1.0 + jnp.tanh(c * (x + 0.044715 * (x * x * x))))


def _dot(a, b):
    return jnp.dot(a, b, preferred_element_type=F32)


def _full(shape):
    n = len(shape)
    return pl.BlockSpec(shape, lambda *_: (0,) * n)


def _rglru_coeffs(ucf, wg_ref, bg_ref, lam_ref):
    gates = _dot(ucf.astype(BF16), wg_ref[...]) + bg_ref[...]
    r = _sigmoid(gates[:, :D_REC])
    ig = _sigmoid(gates[:, D_REC:])
    log_a = (-RG_C) * r * _softplus(-lam_ref[...])
    a = jnp.exp(log_a)
    bt = jnp.sqrt(-jnp.tanh(log_a) * (a * a + 1.0)) * (ig * ucf)
    return a, bt


def _mix_prompt_kernel(x_ref, win_ref, bin_ref, wca_ref, bca_ref, wg_ref, bg_ref, lam_ref,
                       wcb_ref, bcb_ref, lcg_ref, lcb_ref, wout_ref, l1g_ref, l1b_ref,
                       x1_ref, hl_ref, ca_ref, cb_ref,
                       exta, extb, a_s, b_s, h_s, hc, *, nb, tl):
    i = pl.program_id(0)
    m = nb * tl
    tlp = tl + SCAN_PAD
    pa = SUBLANES
    pb = 4 * SUBLANES
    wa = REC_CONV_W
    wb = CONF_CONV_W

    @pl.when(i == 0)
    def _():
        exta[:, 0:pa, :] = jnp.zeros((nb, pa, D_REC), F32)
        extb[:, 0:pb, :] = jnp.zeros((nb, pb, D_CONV), F32)
        hc[...] = jnp.zeros((nb, D_REC), F32)

    xf = x_ref[...].reshape(m, D_MODEL)
    z = _dot(xf.astype(BF16), win_ref[...]) + bin_ref[...]
    g = z[:, 0:D_REC]
    u = z[:, D_REC:2 * D_REC]
    v = z[:, 2 * D_REC:2 * D_REC + D_CONV]
    gg = z[:, 2 * D_REC + D_CONV:]

    exta[:, pa:pa + tl, :] = u.reshape(nb, tl, D_REC)
    uc = jnp.zeros((nb, tl, D_REC), F32) + bca_ref[...]
    for j in range(wa):
        s0 = pa - (wa - 1) + j
        uc = uc + wca_ref[j:j + 1, :] * exta[:, s0:s0 + tl, :]
    tail_a = exta[:, pa + tl - (wa - 1):pa + tl, :]
    ca_ref[...] = tail_a
    exta[:, pa - (wa - 1):pa, :] = tail_a

    ucf = uc.reshape(m, D_REC)
    a, bt = _rglru_coeffs(ucf, wg_ref, bg_ref, lam_ref)

    nlt = D_REC // LANES
    for n in range(nb):
        for c in range(nlt):
            a_s[c, pl.ds(n * tlp, tl), :] = a[n * tl:(n + 1) * tl, c * LANES:(c + 1) * LANES]
            b_s[c, pl.ds(n * tlp, tl), :] = bt[n * tl:(n + 1) * tl, c * LANES:(c + 1) * LANES]

    def step(t, hs):
        out = []
        for c in range(nlt):
            hn = a_s[c, pl.ds(t, nb, stride=tlp), :] * hs[c] + b_s[c, pl.ds(t, nb, stride=tlp), :]
            h_s[c, pl.ds(t, nb, stride=tlp), :] = hn
            out.append(hn)
        return tuple(out)

    hs = lax.fori_loop(0, tl, step, tuple(hc[:, c * LANES:(c + 1) * LANES] for c in range(nlt)))
    h = jnp.concatenate(hs, axis=-1)
    hc[...] = h
    hl_ref[...] = h
    hseq = jnp.concatenate(
        [jnp.concatenate([h_s[c, pl.ds(n * tlp, tl), :] for c in range(nlt)], axis=-1) for n in range(nb)],
        axis=0)
    y_a = _gelu_tanh(g) * hseq

    glu = v * _sigmoid(gg)
    extb[:, pb:pb + tl, :] = glu.reshape(nb, tl, D_CONV)
    c = jnp.zeros((nb, tl, D_CONV), F32) + bcb_ref[...]
    for j in range(wb):
        s0 = pb - (wb - 1) + j
        c = c + wcb_ref[j:j + 1, :] * extb[:, s0:s0 + tl, :]
    tail_b = extb[:, pb + tl - (wb - 1):pb + tl, :]
    cb_ref[...] = tail_b
    extb[:, pb - (wb - 1):pb, :] = tail_b
    y_b = _silu(_ln(c.reshape(m, D_CONV), lcg_ref[...], lcb_ref[...]))

    mix = jnp.concatenate([y_a, y_b], axis=-1).astype(BF16)
    o = _dot(mix, wout_ref[...])
    x1 = _ln(ALPHA * xf + o, l1g_ref[...], l1b_ref[...])
    x1_ref[...] = x1.reshape(nb, tl, D_MODEL)


def _mix_prompt(x, wp):
    nb, L, _ = x.shape
    tl = MIX_TL
    tlp = tl + SCAN_PAD
    kern = functools.partial(_mix_prompt_kernel, nb=nb, tl=tl)
    ins = [wp['w_in'], wp['b_in'], wp['w_conv_a'], wp['b_conv_a'], wp['w_gate'], wp['b_gate'],
           wp['lam'], wp['w_conv_b'], wp['b_conv_b'], wp['ln_c_g'], wp['ln_c_b'], wp['w_out'],
           wp['ln1_g'], wp['ln1_b']]
    return pl.pallas_call(
        kern,
        grid=(L // tl,),
        in_specs=[pl.BlockSpec((nb, tl, D_MODEL), lambda i: (0, i, 0))] + [_full(a.shape) for a in ins],
        out_specs=[pl.BlockSpec((nb, tl, D_MODEL), lambda i: (0, i, 0)),
                   _full((nb, D_REC)), _full((nb, REC_CONV_W - 1, D_REC)),
                   _full((nb, CONF_CONV_W - 1, D_CONV))],
        out_shape=[jax.ShapeDtypeStruct((nb, L, D_MODEL), F32),
                   jax.ShapeDtypeStruct((nb, D_REC), F32),
                   jax.ShapeDtypeStruct((nb, REC_CONV_W - 1, D_REC), F32),
                   jax.ShapeDtypeStruct((nb, CONF_CONV_W - 1, D_CONV), F32)],
        scratch_shapes=[pltpu.VMEM((nb, SUBLANES + tl, D_REC), F32),
                        pltpu.VMEM((nb, 4 * SUBLANES + tl, D_CONV), F32),
                        pltpu.VMEM((D_REC // LANES, nb * tlp, LANES), F32),
                        pltpu.VMEM((D_REC // LANES, nb * tlp, LANES), F32),
                        pltpu.VMEM((D_REC // LANES, nb * tlp, LANES), F32),
                        pltpu.VMEM((nb, D_REC), F32)],
        compiler_params=_cp(("arbitrary",)),
        name="mix_prompt",
    )(x, *ins)


def _mix_sample_kernel(x_ref, sta_ref, stb_ref, h0_ref, win_ref, bin_ref, wca_ref, bca_ref, wg_ref,
                       bg_ref, lam_ref, wcb_ref, bcb_ref, lcg_ref, lcb_ref, wout_ref, l1g_ref, l1b_ref,
                       x1_ref, h_ref, u_ref, glu_ref):
    wa = REC_CONV_W
    wb = CONF_CONV_W
    xf = x_ref[...]
    z = _dot(xf.astype(BF16), win_ref[...]) + bin_ref[...]
    g = z[:, 0:D_REC]
    u = z[:, D_REC:2 * D_REC]
    v = z[:, 2 * D_REC:2 * D_REC + D_CONV]
    gg = z[:, 2 * D_REC + D_CONV:]

    uc = bca_ref[...] + wca_ref[wa - 1:wa, :] * u
    for j in range(wa - 1):
        uc = uc + wca_ref[j:j + 1, :] * sta_ref[j]
    a, bt = _rglru_coeffs(uc, wg_ref, bg_ref, lam_ref)
    h = a * h0_ref[...] + bt
    y_a = _gelu_tanh(g) * h

    glu = v * _sigmoid(gg)
    c = bcb_ref[...] + wcb_ref[wb - 1:wb, :] * glu
    for j in range(wb - 1):
        c = c + wcb_ref[j:j + 1, :] * stb_ref[j]
    y_b = _silu(_ln(c, lcg_ref[...], lcb_ref[...]))

    mix = jnp.concatenate([y_a, y_b], axis=-1).astype(BF16)
    o = _dot(mix, wout_ref[...])
    x1_ref[...] = _ln(ALPHA * xf + o, l1g_ref[...], l1b_ref[...])
    h_ref[...] = h
    u_ref[...] = u
    glu_ref[...] = glu


def _mix_sample(x, sta, stb, h0, wp):
    n = x.shape[0]
    ins = [x, sta, stb, h0, wp['w_in'], wp['b_in'], wp['w_conv_a'], wp['b_conv_a'], wp['w_gate'],
           wp['b_gate'], wp['lam'], wp['w_conv_b'], wp['b_conv_b'], wp['ln_c_g'], wp['ln_c_b'],
           wp['w_out'], wp['ln1_g'], wp['ln1_b']]
    return pl.pallas_call(
        _mix_sample_kernel,
        grid=(1,),
        in_specs=[_full(a.shape) for a in ins],
        out_specs=[_full((n, D_MODEL)), _full((n, D_REC)), _full((n, D_REC)), _full((n, D_CONV))],
        out_shape=[jax.ShapeDtypeStruct((n, D_MODEL), F32), jax.ShapeDtypeStruct((n, D_REC), F32),
                   jax.ShapeDtypeStruct((n, D_REC), F32), jax.ShapeDtypeStruct((n, D_CONV), F32)],
        compiler_params=_cp(("arbitrary",)),
        name="mix_sample",
    )(*ins)


def _linear_kernel(a_ref, w_ref, o_ref):
    o_ref[...] = _dot(a_ref[...].astype(BF16), w_ref[...])


def _linear(a, w, tm, name):
    m, k = a.shape
    n = w.shape[1]
    return pl.pallas_call(
        _linear_kernel,
        grid=(m // tm,),
        in_specs=[pl.BlockSpec((tm, k), lambda i: (i, 0)), _full(w.shape)],
        out_specs=pl.BlockSpec((tm, n), lambda i: (i, 0)),
        out_shape=jax.ShapeDtypeStruct((m, n), F32),
        compiler_params=_cp(("parallel",)),
        name=name,
    )(a, w)


def _linear_res_ln_kernel(a_ref, r_ref, w_ref, g_ref, b_ref, o_ref):
    o = _dot(a_ref[...].astype(BF16), w_ref[...])
    o_ref[...] = _ln(ALPHA * r_ref[...] + o, g_ref[...], b_ref[...])


def _linear_res_ln(a, res, w, g, b, tm, name):
    m, k = a.shape
    n = w.shape[1]
    return pl.pallas_call(
        _linear_res_ln_kernel,
        grid=(m // tm,),
        in_specs=[pl.BlockSpec((tm, k), lambda i: (i, 0)), pl.BlockSpec((tm, n), lambda i: (i, 0)),
                  _full(w.shape), _full(g.shape), _full(b.shape)],
        out_specs=pl.BlockSpec((tm, n), lambda i: (i, 0)),
        out_shape=jax.ShapeDtypeStruct((m, n), F32),
        compiler_params=_cp(("parallel",)),
        name=name,
    )(a, res, w, g, b)


def _attn_prompt_kernel(x_ref, k_ref, v_ref, wq_ref, wo_ref, g_ref, b_ref, o_ref):
    x = x_ref[0]
    q = _dot(x.astype(BF16), wq_ref[...])
    k = k_ref[0].astype(BF16)
    v = v_ref[0].astype(BF16)
    scale = XA_HEAD_DIM ** -0.5
    outs = []
    for h in range(XA_HEADS):
        sl = slice(h * XA_HEAD_DIM, (h + 1) * XA_HEAD_DIM)
        s = lax.dot_general(q[:, sl].astype(BF16), k[:, sl], (((1,), (1,)), ((), ())),
                            preferred_element_type=F32) * scale
        e = jnp.exp(s - jnp.max(s, axis=-1, keepdims=True))
        p = e / jnp.sum(e, axis=-1, keepdims=True)
        outs.append(_dot(p.astype(BF16), v[:, sl]))
    o = jnp.concatenate(outs, axis=-1).astype(BF16)
    att = _dot(o, wo_ref[...])
    o_ref[0] = _ln(ALPHA * x + att, g_ref[...], b_ref[...])


def _attn_prompt(x1, mk, mv, wq, wo, g, b):
    nb, L, d = x1.shape
    nm = mk.shape[1]
    tm = ATT_TM
    return pl.pallas_call(
        _attn_prompt_kernel,
        grid=(nb, L // tm),
        in_specs=[pl.BlockSpec((1, tm, d), lambda bi, i: (bi, i, 0)),
                  pl.BlockSpec((1, nm, d), lambda bi, i: (bi, 0, 0)),
                  pl.BlockSpec((1, nm, d), lambda bi, i: (bi, 0, 0)),
                  _full(wq.shape), _full(wo.shape), _full(g.shape), _full(b.shape)],
        out_specs=pl.BlockSpec((1, tm, d), lambda bi, i: (bi, i, 0)),
        out_shape=jax.ShapeDtypeStruct((nb, L, d), F32),
        compiler_params=_cp(("parallel", "parallel")),
        name="attn_prompt",
    )(x1, mk, mv, wq, wo, g, b)


def _attn_sample_kernel(q_ref, k_ref, v_ref, o_ref):
    scale = XA_HEAD_DIM ** -0.5
    for gi in range(ATT_G):
        qn = q_ref[gi]
        s = jnp.sum(k_ref[gi] * qn[None], axis=-1, keepdims=True) * scale
        e = jnp.exp(s - jnp.max(s, axis=0, keepdims=True))
        p = e / jnp.sum(e, axis=0, keepdims=True)
        o_ref[gi] = jnp.sum(p * v_ref[gi], axis=0)


def _attn_sample(q, ck, cv):
    _, n, nm, nh, dh = ck.shape
    q3 = q.reshape(n, nh, dh)
    cache_spec = pl.BlockSpec((None, ATT_G, nm, nh, dh), lambda i: (0, i, 0, 0, 0))
    out = pl.pallas_call(
        _attn_sample_kernel,
        grid=(n // ATT_G,),
        in_specs=[pl.BlockSpec((ATT_G, nh, dh), lambda i: (i, 0, 0)), cache_spec, cache_spec],
        out_specs=pl.BlockSpec((ATT_G, nh, dh), lambda i: (i, 0, 0)),
        out_shape=jax.ShapeDtypeStruct((n, nh, dh), F32),
        compiler_params=_cp(("parallel",)),
        name="attn_sample",
    )(q3, ck, cv)
    return out.reshape(n, nh * dh)


def _first_argmax(val, iota, n, axis):
    mx = jnp.max(val, axis=axis, keepdims=True)
    idx = jnp.min(jnp.where(val == mx, iota, n), axis=axis, keepdims=True)
    return mx, idx


def _router_kernel(x_ref, wr_ref, rb_ref, idx_ref, gate_ref, rank_ref, cnt_ref, carry):
    tm = x_ref.shape[0]
    neg = -jnp.inf

    @pl.when(pl.program_id(1) == 0)
    def _():
        carry[...] = jnp.zeros(carry.shape, F32)

    logits = lax.dot_general(wr_ref[...], x_ref[...], (((1,), (1,)), ((), ())),
                             precision=lax.Precision.HIGHEST, preferred_element_type=F32)
    scores = _sigmoid(logits)
    biased = scores + rb_ref[...]
    grp = biased.reshape(N_GROUPS, GROUP_SIZE, tm)
    io_g = lax.broadcasted_iota(jnp.int32, (N_GROUPS, GROUP_SIZE, tm), 1)
    m1, i1 = _first_argmax(grp, io_g, GROUP_SIZE, 1)
    m2 = jnp.max(jnp.where(io_g == i1, neg, grp), axis=1, keepdims=True)
    gscore = (m1 + m2).reshape(N_GROUPS, tm)
    io_n = lax.broadcasted_iota(jnp.int32, (N_GROUPS, tm), 0)
    gsel = jnp.zeros((N_GROUPS, tm), jnp.bool_)
    for _ in range(TOPK_GROUPS):
        _, gi = _first_argmax(gscore, io_n, N_GROUPS, 0)
        hit = io_n == gi
        gsel = jnp.logical_or(gsel, hit)
        gscore = jnp.where(hit, neg, gscore)
    emask = jnp.broadcast_to(gsel.reshape(N_GROUPS, 1, tm), (N_GROUPS, GROUP_SIZE, tm)).reshape(N_EXPERTS, tm)
    val = jnp.where(emask, biased, neg)
    io_e = lax.broadcasted_iota(jnp.int32, (N_EXPERTS, tm), 0)
    idxs, ws, hits = [], [], []
    for _ in range(TOP_K):
        _, ei = _first_argmax(val, io_e, N_EXPERTS, 0)
        hit = io_e == ei
        ws.append(jnp.sum(jnp.where(hit, scores, 0.0), axis=0, keepdims=True))
        idxs.append(ei)
        hits.append(hit)
        val = jnp.where(hit, neg, val)
    w = jnp.concatenate(ws, axis=0)
    idx_ref[...] = jnp.concatenate(idxs, axis=0)
    gate_ref[...] = w / jnp.sum(w, axis=0, keepdims=True) * ROUTED_SCALE

    onehot = jnp.zeros((N_EXPERTS, tm), F32)
    for hit in hits:
        onehot = onehot + hit.astype(F32)
    tri = (lax.broadcasted_iota(jnp.int32, (tm, tm), 0) <= lax.broadcasted_iota(jnp.int32, (tm, tm), 1))
    csum = _dot(onehot.astype(BF16), tri.astype(BF16)) + carry[...]
    ranks = [jnp.sum(jnp.where(hit, csum, 0.0), axis=0, keepdims=True) - 1.0 for hit in hits]
    rank_ref[...] = jnp.concatenate(ranks, axis=0).astype(jnp.int32)
    total = csum[:, tm - 1:tm]
    carry[...] = total
    cnt_ref[0] = total


def _router(x2, wr_t, rb, nseg):
    t, d = x2.shape
    tm = LANES
    tiles = t // (nseg * tm)
    tok_spec = pl.BlockSpec((TOP_K, tm), lambda s, j: (0, s * tiles + j))
    return pl.pallas_call(
        _router_kernel,
        grid=(nseg, tiles),
        in_specs=[pl.BlockSpec((tm, d), lambda s, j: (s * tiles + j, 0)), _full(wr_t.shape), _full(rb.shape)],
        out_specs=[tok_spec, tok_spec, tok_spec, pl.BlockSpec((1, N_EXPERTS, 1), lambda s, j: (s, 0, 0))],
        out_shape=[jax.ShapeDtypeStruct((TOP_K, t), jnp.int32), jax.ShapeDtypeStruct((TOP_K, t), F32),
                   jax.ShapeDtypeStruct((TOP_K, t), jnp.int32),
                   jax.ShapeDtypeStruct((nseg, N_EXPERTS, 1), F32)],
        scratch_shapes=[pltpu.VMEM((N_EXPERTS, 1), F32)],
        compiler_params=_cp(("arbitrary", "arbitrary")),
        name="router",
    )(x2, wr_t, rb)


def _dest_kernel(idx_ref, rank_ref, cnt_ref, dest_ref, be_ref, bn_ref):
    tm = idx_ref.shape[1]
    r = float(MOE_R)
    counts = cnt_ref[0]
    padded = jnp.floor((counts + (r - 1.0)) / r) * r
    below = (lax.broadcasted_iota(jnp.int32, (N_EXPERTS, N_EXPERTS), 1)
             < lax.broadcasted_iota(jnp.int32, (N_EXPERTS, N_EXPERTS), 0)).astype(F32)
    pstart = jnp.dot(below, jnp.broadcast_to(padded, (N_EXPERTS, LANES)), precision=lax.Precision.HIGHEST,
                     preferred_element_type=F32)[:, 0:1]
    pend = pstart + padded
    io_e = lax.broadcasted_iota(jnp.int32, (N_EXPERTS, tm), 0)
    dests = []
    for k in range(TOP_K):
        hit = io_e == idx_ref[k:k + 1, :]
        ps = jnp.sum(jnp.where(hit, pstart, 0.0), axis=0, keepdims=True)
        dests.append(ps + rank_ref[k:k + 1, :].astype(F32))
    dest_ref[...] = jnp.concatenate(dests, axis=0).astype(jnp.int32)

    nb = be_ref.shape[-1]
    bpos = lax.broadcasted_iota(jnp.int32, (1, nb), 1).astype(F32) * r
    be = jnp.minimum(jnp.sum((bpos >= pend).astype(F32), axis=0, keepdims=True), float(N_EXPERTS - 1))
    sel = lax.broadcasted_iota(jnp.int32, (N_EXPERTS, nb), 0) == be.astype(jnp.int32)
    cnt_b = jnp.sum(jnp.where(sel, counts, 0.0), axis=0, keepdims=True)
    st_b = jnp.sum(jnp.where(sel, pstart, 0.0), axis=0, keepdims=True)
    be_ref[0] = be.astype(jnp.int32)
    bn_ref[0] = jnp.clip(cnt_b - (bpos - st_b), 0.0, r).astype(jnp.int32)


def _dest(idx_t, rank_t, counts, nseg, nblk_pad):
    t = idx_t.shape[1]
    tm = LANES
    tiles = t // (nseg * tm)
    tok_spec = pl.BlockSpec((TOP_K, tm), lambda s, j: (0, s * tiles + j))
    tab_spec = pl.BlockSpec((1, 1, nblk_pad), lambda s, j: (s, 0, 0))
    return pl.pallas_call(
        _dest_kernel,
        grid=(nseg, tiles),
        in_specs=[tok_spec, tok_spec, pl.BlockSpec((1, N_EXPERTS, 1), lambda s, j: (s, 0, 0))],
        out_specs=[tok_spec, tab_spec, tab_spec],
        out_shape=[jax.ShapeDtypeStruct((TOP_K, t), jnp.int32),
                   jax.ShapeDtypeStruct((nseg, 1, nblk_pad), jnp.int32),
                   jax.ShapeDtypeStruct((nseg, 1, nblk_pad), jnp.int32)],
        compiler_params=_cp(("arbitrary", "arbitrary")),
        name="dest",
    )(idx_t, rank_t, counts)


def _invert_kernel(dest_ref, slot_ref, *, tseg, nslots):
    def init(i, c):
        for j in range(SUBLANES):
            slot_ref[i * SUBLANES + j] = tseg * TOP_K
        return c

    lax.fori_loop(0, nslots // SUBLANES, init, 0)

    def body(t, c):
        for k in range(TOP_K):
            slot_ref[dest_ref[k * tseg + t]] = t * TOP_K + k
        return c

    lax.fori_loop(0, tseg, body, 0)


def _invert(dest_flat, tseg, nslots):
    grid_spec = pltpu.PrefetchScalarGridSpec(
        num_scalar_prefetch=1, grid=(1,), in_specs=[],
        out_specs=pl.BlockSpec(memory_space=pltpu.SMEM))
    return pl.pallas_call(
        functools.partial(_invert_kernel, tseg=tseg, nslots=nslots),
        grid_spec=grid_spec,
        out_shape=jax.ShapeDtypeStruct((nslots,), jnp.int32),
        compiler_params=_cp(("arbitrary",)),
        name="invert",
    )(dest_flat)


def _expert_kernel(be_ref, bn_ref, slot_ref, gate_ref, x_ref, w13_ref, w2_ref, acc_ref, tile, ys, *, tseg):
    del be_ref
    b = pl.program_id(0)
    r = MOE_R
    s = TILE_S
    zero_rows = 64 * ROW_SLAB
    dummy_off = tseg * ROW_SLAB

    @pl.when(b == 0)
    def _():
        def zero(i, c):
            acc_ref[pl.ds(pl.multiple_of(i * zero_rows, zero_rows), zero_rows), :] = jnp.zeros((zero_rows, LANES), F32)
            return c

        lax.fori_loop(0, acc_ref.shape[0] // zero_rows, zero, 0)
        rem = acc_ref.shape[0] % zero_rows
        if rem:
            acc_ref[pl.ds(acc_ref.shape[0] - rem, rem), :] = jnp.zeros((rem, LANES), F32)
        tile[...] = jnp.zeros(tile.shape, F32)

    n = bn_ref[b]

    @pl.when(n > 0)
    def _():
        base = b * r
        nchunk = (n + (SUBLANES - 1)) // SUBLANES

        def row_info(i):
            v = slot_ref[base + jnp.minimum(i, n - 1)]
            return v, pl.multiple_of((v >> 3) << 3, ROW_SLAB)

        def gather(q, carry):
            for j in range(SUBLANES):
                i = q * SUBLANES + j
                _, off = row_info(i)
                tile[pl.ds(i, ROW_SLAB, stride=s), :] = x_ref[pl.ds(off, ROW_SLAB), :]
            return carry

        lax.fori_loop(0, nchunk, gather, 0)
        xb = jnp.concatenate([tile[c * s:c * s + r, :] for c in range(ROW_SLAB)], axis=-1).astype(BF16)
        h13 = _dot(xb, w13_ref[0])
        hh = (_silu(h13[:, :D_EXPERT]) * h13[:, D_EXPERT:]).astype(BF16)
        y = _dot(hh, w2_ref[0])
        for g in range(r // SUBLANES):
            for c in range(ROW_SLAB):
                ys[pl.ds((g * ROW_SLAB + c) * SUBLANES, SUBLANES), :] = (
                    y[g * SUBLANES:(g + 1) * SUBLANES, c * LANES:(c + 1) * LANES])

        def scatter(q, carry):
            for half in range(2):
                offs, vals = [], []
                for j in range(half * 4, half * 4 + 4):
                    i = q * SUBLANES + j
                    valid = i < n
                    v, off = row_info(i)
                    off = pl.multiple_of(jnp.where(valid, off, dummy_off), ROW_SLAB)
                    gte = jnp.where(valid, gate_ref[v], 0.0)
                    src = q * (ROW_SLAB * SUBLANES) + j
                    vals.append(acc_ref[pl.ds(off, ROW_SLAB), :]
                                + ys[pl.ds(src, ROW_SLAB, stride=SUBLANES), :] * gte)
                    offs.append(off)
                for off, val in zip(offs, vals):
                    acc_ref[pl.ds(off, ROW_SLAB), :] = val
            return carry

        lax.fori_loop(0, nchunk, scatter, 0)


def _expert_segment(blk_e, blk_n, slots, gates, xslab, w13, w2, tseg):
    nblk = -(-(tseg * TOP_K + N_EXPERTS * (MOE_R - 1)) // MOE_R)
    r = MOE_R
    acc_rows = (tseg + SUBLANES) * ROW_SLAB
    one = pl.Buffered(1)
    grid_spec = pltpu.PrefetchScalarGridSpec(
        num_scalar_prefetch=4,
        grid=(nblk,),
        in_specs=[pl.BlockSpec(xslab.shape, lambda b, *_: (0, 0), pipeline_mode=one),
                  pl.BlockSpec((1, D_MODEL, 2 * D_EXPERT), lambda b, be, *_: (be[b], 0, 0)),
                  pl.BlockSpec((1, D_EXPERT, D_MODEL), lambda b, be, *_: (be[b], 0, 0))],
        out_specs=pl.BlockSpec((acc_rows, LANES), lambda b, *_: (0, 0), pipeline_mode=one),
        scratch_shapes=[pltpu.VMEM((ROW_SLAB * TILE_S, LANES), F32),
                        pltpu.VMEM((r * ROW_SLAB, LANES), F32)],
    )
    return pl.pallas_call(
        functools.partial(_expert_kernel, tseg=tseg),
        grid_spec=grid_spec,
        out_shape=jax.ShapeDtypeStruct((acc_rows, LANES), F32),
        compiler_params=_cp(("arbitrary",)),
        name="experts",
    )(blk_e, blk_n, slots, gates, xslab, w13, w2)


def _final_kernel(x_ref, r_ref, w13_ref, w2_ref, g_ref, b_ref, o_ref):
    x = x_ref[...]
    h13 = _dot(x.astype(BF16), w13_ref[...])
    hh = (_silu(h13[:, :D_EXPERT]) * h13[:, D_EXPERT:]).astype(BF16)
    sh = _dot(hh, w2_ref[...])
    o_ref[...] = _ln(ALPHA * x + (sh + r_ref[...]), g_ref[...], b_ref[...])


def _final(x2, routed, w13, w2, g, b):
    t, d = x2.shape
    tm = TOK_TM
    return pl.pallas_call(
        _final_kernel,
        grid=(t // tm,),
        in_specs=[pl.BlockSpec((tm, d), lambda i: (i, 0)), pl.BlockSpec((tm, d), lambda i: (i, 0)),
                  _full(w13.shape), _full(w2.shape), _full(g.shape), _full(b.shape)],
        out_specs=pl.BlockSpec((tm, d), lambda i: (i, 0)),
        out_shape=jax.ShapeDtypeStruct((t, d), F32),
        compiler_params=_cp(("parallel",)),
        name="final",
    )(x2, routed, w13, w2, g, b)


def _block_diag_dense(w):
    h, dh, _ = w.shape
    eye = jnp.eye(h, dtype=w.dtype)
    return jnp.einsum('hij,hg->higj', w, eye).reshape(h * dh, h * dh)


def kernel(x_prompt, x_sample, mem_prompt, state_rglru_h, state_rglru_conv, state_dwconv, cache_mem_k, cache_mem_v, w_in, b_in, w_conv_a, b_conv_a, w_rg_a, b_rg_a, w_rg_x, b_rg_x, rg_lambda, w_conv_b, b_conv_b, ln_c_g, ln_c_b, w_out, ln1_g, ln1_b, w_q, w_k, w_v, w_o, ln2_g, ln2_b, w_router, router_bias, w1_e, w3_e, w2_e, w1_s, w3_s, w2_s, ln3_g, ln3_b):
    assert w_in.shape[0] == 1, "single-layer trunk"
    nb, L, d = x_prompt.shape
    ns = x_sample.shape[0]
    nm = mem_prompt.shape[1]
    row = lambda a: a.reshape(1, -1)

    wp = dict(
        w_in=w_in[0].astype(BF16), b_in=row(b_in[0]),
        w_conv_a=w_conv_a[0], b_conv_a=row(b_conv_a[0]),
        w_gate=jnp.concatenate([_block_diag_dense(w_rg_a[0]), _block_diag_dense(w_rg_x[0])], axis=1).astype(BF16),
        b_gate=row(jnp.concatenate([b_rg_a[0], b_rg_x[0]])),
        lam=row(rg_lambda[0]),
        w_conv_b=w_conv_b[0], b_conv_b=row(b_conv_b[0]),
        ln_c_g=row(ln_c_g[0]), ln_c_b=row(ln_c_b[0]),
        w_out=w_out[0].astype(BF16), ln1_g=row(ln1_g[0]), ln1_b=row(ln1_b[0]),
    )
    wq = w_q[0].astype(BF16)
    wo = w_o[0].astype(BF16)
    wkv = jnp.concatenate([w_k[0], w_v[0]], axis=1).astype(BF16)
    l2g, l2b = row(ln2_g[0]), row(ln2_b[0])

    x1_p, h_p, ca_p, cb_p = _mix_prompt(x_prompt, wp)
    kv = _linear(mem_prompt.reshape(nb * nm, d), wkv, 512, "kv_proj")
    mk_p = kv[:, :d].reshape(nb, nm, d)
    mv_p = kv[:, d:].reshape(nb, nm, d)
    x2_p = _attn_prompt(x1_p, mk_p, mv_p, wq, wo, l2g, l2b)

    xs = x_sample.reshape(ns, d)
    sta = jnp.transpose(state_rglru_conv[0], (1, 0, 2))
    stb = jnp.transpose(state_dwconv[0], (1, 0, 2))
    x1_s, h_s, u_s, glu_s = _mix_sample(xs, sta, stb, state_rglru_h[0], wp)
    q_s = _linear(x1_s, wq, ns, "q_sample")
    o_s = _attn_sample(q_s, cache_mem_k, cache_mem_v)
    x2_s = _linear_res_ln(o_s, x1_s, wo, l2g, l2b, ns, "attn_out_sample")

    x2 = jnp.concatenate([x2_p.reshape(nb * L, d), x2_s], axis=0)
    t = x2.shape[0]
    nseg = MOE_NSEG
    tseg = t // nseg
    assert tseg * nseg == t and tseg % LANES == 0 and t % TOK_TM == 0
    nblk = -(-(tseg * TOP_K + N_EXPERTS * (MOE_R - 1)) // MOE_R)
    nblk_pad = -(-nblk // LANES) * LANES
    idx_t, gate_t, rank_t, counts = _router(x2, jnp.transpose(w_router[0]),
                                            router_bias[0].reshape(N_EXPERTS, 1), nseg)
    dest_t, blk_e, blk_n = _dest(idx_t, rank_t, counts, nseg, nblk_pad)
    w13_e = jnp.concatenate([w1_e[0], w3_e[0]], axis=-1).astype(BF16)
    w2_eb = w2_e[0].astype(BF16)
    xslab = x2.reshape(nseg, tseg * ROW_SLAB, LANES)
    parts = []
    for si in range(nseg):
        seg = slice(si * tseg, (si + 1) * tseg)
        slots = _invert(dest_t[:, seg].reshape(-1), tseg, nblk * MOE_R)
        gates = jnp.transpose(gate_t[:, seg]).reshape(-1)
        acc = _expert_segment(blk_e[si, 0], blk_n[si, 0], slots, gates, xslab[si], w13_e, w2_eb, tseg)
        parts.append(acc[:tseg * ROW_SLAB].reshape(tseg, d))
    routed = jnp.concatenate(parts, axis=0)
    w13_s = jnp.concatenate([w1_s[0], w3_s[0]], axis=-1).astype(BF16)
    y = _final(x2, routed, w13_s, w2_s[0].astype(BF16), row(ln3_g[0]), row(ln3_b[0]))

    y_p = y[:nb * L].reshape(nb, L, d)
    y_s = y[nb * L:].reshape(ns, 1, d)
    new_ca_s = jnp.concatenate([state_rglru_conv[0][:, 1:], u_s[:, None, :]], axis=1)
    new_cb_s = jnp.concatenate([state_dwconv[0][:, 1:], glu_s[:, None, :]], axis=1)
    hd = XA_HEAD_DIM
    return (y_p, y_s, h_p[None], ca_p[None], cb_p[None],
            mk_p.reshape(1, nb, nm, XA_HEADS, hd), mv_p.reshape(1, nb, nm, XA_HEADS, hd),
            h_s[None], new_ca_s[None], new_cb_s[None])
```

```python
import functools

import jax
import jax.numpy as jnp
from jax import lax
from jax.experimental import pallas as pl
from jax.experimental.pallas import tpu as pltpu
from jax.experimental.pallas import tpu_sc as plsc

F32 = jnp.float32
BF16 = jnp.bfloat16

D_MODEL = 1024
D_REC = 512
D_CONV = 512
REC_HEADS = 8
REC_HEAD_DIM = D_REC // REC_HEADS
REC_CONV_W = 4
RG_C = 8.0
CONF_CONV_W = 31
XA_HEADS = 4
XA_HEAD_DIM = D_MODEL // XA_HEADS
N_EXPERTS = 64
TOP_K = 8
N_GROUPS = 8
TOPK_GROUPS = 4
GROUP_SIZE = N_EXPERTS // N_GROUPS
D_EXPERT = D_MODEL // 4
ROUTED_SCALE = 2.5
LN_EPS = 1e-5
DEPTH = 1
ALPHA = (2 * DEPTH) ** 0.25

LANES = 128
SUBLANES = 8
VMEM_LIMIT = 56 * 1024 * 1024

MIX_TL = 64
SCAN_PAD = 8
ATT_TM = 512
ATT_G = 4
TOK_TM = 384
MOE_NSEG = 3
MOE_R = 128
ROW_SLAB = D_MODEL // LANES
TILE_S = MOE_R + 8


def _cp(sem=None):
    return pltpu.CompilerParams(dimension_semantics=sem, vmem_limit_bytes=VMEM_LIMIT)


def _ln(x, g, b):
    mu = jnp.mean(x, axis=-1, keepdims=True)
    xc = x - mu
    var = jnp.mean(xc * xc, axis=-1, keepdims=True)
    return xc * lax.rsqrt(var + LN_EPS) * g + b


def _sigmoid(x):
    return 1.0 / (1.0 + jnp.exp(-x))


def _silu(x):
    return x * _sigmoid(x)


def _softplus(x):
    return jnp.maximum(x, 0.0) + jnp.log1p(jnp.exp(-jnp.abs(x)))


def _gelu_tanh(x):
    c = 0.7978845608028654
    return 0.5 * x * (1.0 + jnp.tanh(c * (x + 0.044715 * (x * x * x))))


def _dot(a, b):
    return jnp.dot(a, b, preferred_element_type=F32)


def _full(shape):
    n = len(shape)
    return pl.BlockSpec(shape, lambda *_: (0,) * n)


def _rglru_coeffs(ucf, wg_ref, bg_ref, lam_ref):
    gates = _dot(ucf.astype(BF16), wg_ref[...]) + bg_ref[...]
    r = _sigmoid(gates[:, :D_REC])
    ig = _sigmoid(gates[:, D_REC:])
    log_a = (-RG_C) * r * _softplus(-lam_ref[...])
    a = jnp.exp(log_a)
    bt = jnp.sqrt(-jnp.tanh(log_a) * (a * a + 1.0)) * (ig * ucf)
    return a, bt


def _mix_prompt_kernel(x_ref, win_ref, bin_ref, wca_ref, bca_ref, wg_ref, bg_ref, lam_ref,
                       wcb_ref, bcb_ref, lcg_ref, lcb_ref, wout_ref, l1g_ref, l1b_ref,
                       x1_ref, hl_ref, ca_ref, cb_ref,
                       exta, extb, a_s, b_s, h_s, hc, *, nb, tl):
    i = pl.program_id(0)
    m = nb * tl
    tlp = tl + SCAN_PAD
    pa = SUBLANES
    pb = 4 * SUBLANES
    wa = REC_CONV_W
    wb = CONF_CONV_W

    @pl.when(i == 0)
    def _():
        exta[:, 0:pa, :] = jnp.zeros((nb, pa, D_REC), F32)
        extb[:, 0:pb, :] = jnp.zeros((nb, pb, D_CONV), F32)
        hc[...] = jnp.zeros((nb, D_REC), F32)

    xf = x_ref[...].reshape(m, D_MODEL)
    z = _dot(xf.astype(BF16), win_ref[...]) + bin_ref[...]
    g = z[:, 0:D_REC]
    u = z[:, D_REC:2 * D_REC]
    v = z[:, 2 * D_REC:2 * D_REC + D_CONV]
    gg = z[:, 2 * D_REC + D_CONV:]

    exta[:, pa:pa + tl, :] = u.reshape(nb, tl, D_REC)
    uc = jnp.zeros((nb, tl, D_REC), F32) + bca_ref[...]
    for j in range(wa):
        s0 = pa - (wa - 1) + j
        uc = uc + wca_ref[j:j + 1, :] * exta[:, s0:s0 + tl, :]
    tail_a = exta[:, pa + tl - (wa - 1):pa + tl, :]
    ca_ref[...] = tail_a
    exta[:, pa - (wa - 1):pa, :] = tail_a

    ucf = uc.reshape(m, D_REC)
    a, bt = _rglru_coeffs(ucf, wg_ref, bg_ref, lam_ref)

    nlt = D_REC // LANES
    for n in range(nb):
        for c in range(nlt):
            a_s[c, pl.ds(n * tlp, tl), :] = a[n * tl:(n + 1) * tl, c * LANES:(c + 1) * LANES]
            b_s[c, pl.ds(n * tlp, tl), :] = bt[n * tl:(n + 1) * tl, c * LANES:(c + 1) * LANES]

    def step(t, hs):
        out = []
        for c in range(nlt):
            hn = a_s[c, pl.ds(t, nb, stride=tlp), :] * hs[c] + b_s[c, pl.ds(t, nb, stride=tlp), :]
            h_s[c, pl.ds(t, nb, stride=tlp), :] = hn
            out.append(hn)
        return tuple(out)

    hs = lax.fori_loop(0, tl, step, tuple(hc[:, c * LANES:(c + 1) * LANES] for c in range(nlt)))
    h = jnp.concatenate(hs, axis=-1)
    hc[...] = h
    hl_ref[...] = h
    hseq = jnp.concatenate(
        [jnp.concatenate([h_s[c, pl.ds(n * tlp, tl), :] for c in range(nlt)], axis=-1) for n in range(nb)],
        axis=0)
    y_a = _gelu_tanh(g) * hseq

    glu = v * _sigmoid(gg)
    extb[:, pb:pb + tl, :] = glu.reshape(nb, tl, D_CONV)
    c = jnp.zeros((nb, tl, D_CONV), F32) + bcb_ref[...]
    for j in range(wb):
        s0 = pb - (wb - 1) + j
        c = c + wcb_ref[j:j + 1, :] * extb[:, s0:s0 + tl, :]
    tail_b = extb[:, pb + tl - (wb - 1):pb + tl, :]
    cb_ref[...] = tail_b
    extb[:, pb - (wb - 1):pb, :] = tail_b
    y_b = _silu(_ln(c.reshape(m, D_CONV), lcg_ref[...], lcb_ref[...]))

    mix = jnp.concatenate([y_a, y_b], axis=-1).astype(BF16)
    o = _dot(mix, wout_ref[...])
    x1 = _ln(ALPHA * xf + o, l1g_ref[...], l1b_ref[...])
    x1_ref[...] = x1.reshape(nb, tl, D_MODEL)


def _mix_prompt(x, wp):
    nb, L, _ = x.shape
    tl = MIX_TL
    tlp = tl + SCAN_PAD
    kern = functools.partial(_mix_prompt_kernel, nb=nb, tl=tl)
    ins = [wp['w_in'], wp['b_in'], wp['w_conv_a'], wp['b_conv_a'], wp['w_gate'], wp['b_gate'],
           wp['lam'], wp['w_conv_b'], wp['b_conv_b'], wp['ln_c_g'], wp['ln_c_b'], wp['w_out'],
           wp['ln1_g'], wp['ln1_b']]
    return pl.pallas_call(
        kern,
        grid=(L // tl,),
        in_specs=[pl.BlockSpec((nb, tl, D_MODEL), lambda i: (0, i, 0))] + [_full(a.shape) for a in ins],
        out_specs=[pl.BlockSpec((nb, tl, D_MODEL), lambda i: (0, i, 0)),
                   _full((nb, D_REC)), _full((nb, REC_CONV_W - 1, D_REC)),
                   _full((nb, CONF_CONV_W - 1, D_CONV))],
        out_shape=[jax.ShapeDtypeStruct((nb, L, D_MODEL), F32),
                   jax.ShapeDtypeStruct((nb, D_REC), F32),
                   jax.ShapeDtypeStruct((nb, REC_CONV_W - 1, D_REC), F32),
                   jax.ShapeDtypeStruct((nb, CONF_CONV_W - 1, D_CONV), F32)],
        scratch_shapes=[pltpu.VMEM((nb, SUBLANES + tl, D_REC), F32),
                        pltpu.VMEM((nb, 4 * SUBLANES + tl, D_CONV), F32),
                        pltpu.VMEM((D_REC // LANES, nb * tlp, LANES), F32),
                        pltpu.VMEM((D_REC // LANES, nb * tlp, LANES), F32),
                        pltpu.VMEM((D_REC // LANES, nb * tlp, LANES), F32),
                        pltpu.VMEM((nb, D_REC), F32)],
        compiler_params=_cp(("arbitrary",)),
        name="mix_prompt",
    )(x, *ins)


def _mix_sample_kernel(x_ref, sta_ref, stb_ref, h0_ref, win_ref, bin_ref, wca_ref, bca_ref, wg_ref,
                       bg_ref, lam_ref, wcb_ref, bcb_ref, lcg_ref, lcb_ref, wout_ref, l1g_ref, l1b_ref,
                       x1_ref, h_ref, u_ref, glu_ref):
    wa = REC_CONV_W
    wb = CONF_CONV_W
    xf = x_ref[...]
    z = _dot(xf.astype(BF16), win_ref[...]) + bin_ref[...]
    g = z[:, 0:D_REC]
    u = z[:, D_REC:2 * D_REC]
    v = z[:, 2 * D_REC:2 * D_REC + D_CONV]
    gg = z[:, 2 * D_REC + D_CONV:]

    uc = bca_ref[...] + wca_ref[wa - 1:wa, :] * u
    for j in range(wa - 1):
        uc = uc + wca_ref[j:j + 1, :] * sta_ref[j]
    a, bt = _rglru_coeffs(uc, wg_ref, bg_ref, lam_ref)
    h = a * h0_ref[...] + bt
    y_a = _gelu_tanh(g) * h

    glu = v * _sigmoid(gg)
    c = bcb_ref[...] + wcb_ref[wb - 1:wb, :] * glu
    for j in range(wb - 1):
        c = c + wcb_ref[j:j + 1, :] * stb_ref[j]
    y_b = _silu(_ln(c, lcg_ref[...], lcb_ref[...]))

    mix = jnp.concatenate([y_a, y_b], axis=-1).astype(BF16)
    o = _dot(mix, wout_ref[...])
    x1_ref[...] = _ln(ALPHA * xf + o, l1g_ref[...], l1b_ref[...])
    h_ref[...] = h
    u_ref[...] = u
    glu_ref[...] = glu


def _mix_sample(x, sta, stb, h0, wp):
    n = x.shape[0]
    ins = [x, sta, stb, h0, wp['w_in'], wp['b_in'], wp['w_conv_a'], wp['b_conv_a'], wp['w_gate'],
           wp['b_gate'], wp['lam'], wp['w_conv_b'], wp['b_conv_b'], wp['ln_c_g'], wp['ln_c_b'],
           wp['w_out'], wp['ln1_g'], wp['ln1_b']]
    return pl.pallas_call(
        _mix_sample_kernel,
        grid=(1,),
        in_specs=[_full(a.shape) for a in ins],
        out_specs=[_full((n, D_MODEL)), _full((n, D_REC)), _full((n, D_REC)), _full((n, D_CONV))],
        out_shape=[jax.ShapeDtypeStruct((n, D_MODEL), F32), jax.ShapeDtypeStruct((n, D_REC), F32),
                   jax.ShapeDtypeStruct((n, D_REC), F32), jax.ShapeDtypeStruct((n, D_CONV), F32)],
        compiler_params=_cp(("arbitrary",)),
        name="mix_sample",
    )(*ins)


def _linear_kernel(a_ref, w_ref, o_ref):
    o_ref[...] = _dot(a_ref[...].astype(BF16), w_ref[...])


def _linear(a, w, tm, name):
    m, k = a.shape
    n = w.shape[1]
    return pl.pallas_call(
        _linear_kernel,
        grid=(m // tm,),
        in_specs=[pl.BlockSpec((tm, k), lambda i: (i, 0)), _full(w.shape)],
        out_specs=pl.BlockSpec((tm, n), lambda i: (i, 0)),
        out_shape=jax.ShapeDtypeStruct((m, n), F32),
        compiler_params=_cp(("parallel",)),
        name=name,
    )(a, w)


def _linear_res_ln_kernel(a_ref, r_ref, w_ref, g_ref, b_ref, o_ref):
    o = _dot(a_ref[...].astype(BF16), w_ref[...])
    o_ref[...] = _ln(ALPHA * r_ref[...] + o, g_ref[...], b_ref[...])


def _linear_res_ln(a, res, w, g, b, tm, name):
    m, k = a.shape
    n = w.shape[1]
    return pl.pallas_call(
        _linear_res_ln_kernel,
        grid=(m // tm,),
        in_specs=[pl.BlockSpec((tm, k), lambda i: (i, 0)), pl.BlockSpec((tm, n), lambda i: (i, 0)),
                  _full(w.shape), _full(g.shape), _full(b.shape)],
        out_specs=pl.BlockSpec((tm, n), lambda i: (i, 0)),
        out_shape=jax.ShapeDtypeStruct((m, n), F32),
        compiler_params=_cp(("parallel",)),
        name=name,
    )(a, res, w, g, b)


def _attn_prompt_kernel(x_ref, k_ref, v_ref, wq_ref, wo_ref, g_ref, b_ref, o_ref):
    x = x_ref[0]
    q = _dot(x.astype(BF16), wq_ref[...])
    k = k_ref[0].astype(BF16)
    v = v_ref[0].astype(BF16)
    scale = XA_HEAD_DIM ** -0.5
    outs = []
    for h in range(XA_HEADS):
        sl = slice(h * XA_HEAD_DIM, (h + 1) * XA_HEAD_DIM)
        s = lax.dot_general(q[:, sl].astype(BF16), k[:, sl], (((1,), (1,)), ((), ())),
                            preferred_element_type=F32) * scale
        e = jnp.exp(s - jnp.max(s, axis=-1, keepdims=True))
        p = e / jnp.sum(e, axis=-1, keepdims=True)
        outs.append(_dot(p.astype(BF16), v[:, sl]))
    o = jnp.concatenate(outs, axis=-1).astype(BF16)
    att = _dot(o, wo_ref[...])
    o_ref[0] = _ln(ALPHA * x + att, g_ref[...], b_ref[...])


def _attn_prompt(x1, mk, mv, wq, wo, g, b):
    nb, L, d = x1.shape
    nm = mk.shape[1]
    tm = ATT_TM
    return pl.pallas_call(
        _attn_prompt_kernel,
        grid=(nb, L // tm),
        in_specs=[pl.BlockSpec((1, tm, d), lambda bi, i: (bi, i, 0)),
                  pl.BlockSpec((1, nm, d), lambda bi, i: (bi, 0, 0)),
                  pl.BlockSpec((1, nm, d), lambda bi, i: (bi, 0, 0)),
                  _full(wq.shape), _full(wo.shape), _full(g.shape), _full(b.shape)],
        out_specs=pl.BlockSpec((1, tm, d), lambda bi, i: (bi, i, 0)),
        out_shape=jax.ShapeDtypeStruct((nb, L, d), F32),
        compiler_params=_cp(("parallel", "parallel")),
        name="attn_prompt",
    )(x1, mk, mv, wq, wo, g, b)


def _attn_sample_kernel(q_ref, k_ref, v_ref, o_ref):
    scale = XA_HEAD_DIM ** -0.5
    for gi in range(ATT_G):
        qn = q_ref[gi]
        s = jnp.sum(k_ref[gi] * qn[None], axis=-1, keepdims=True) * scale
        e = jnp.exp(s - jnp.max(s, axis=0, keepdims=True))
        p = e / jnp.sum(e, axis=0, keepdims=True)
        o_ref[gi] = jnp.sum(p * v_ref[gi], axis=0)


def _attn_sample(q, ck, cv):
    _, n, nm, nh, dh = ck.shape
    q3 = q.reshape(n, nh, dh)
    cache_spec = pl.BlockSpec((None, ATT_G, nm, nh, dh), lambda i: (0, i, 0, 0, 0))
    out = pl.pallas_call(
        _attn_sample_kernel,
        grid=(n // ATT_G,),
        in_specs=[pl.BlockSpec((ATT_G, nh, dh), lambda i: (i, 0, 0)), cache_spec, cache_spec],
        out_specs=pl.BlockSpec((ATT_G, nh, dh), lambda i: (i, 0, 0)),
        out_shape=jax.ShapeDtypeStruct((n, nh, dh), F32),
        compiler_params=_cp(("parallel",)),
        name="attn_sample",
    )(q3, ck, cv)
    return out.reshape(n, nh * dh)


def _first_argmax(val, iota, n, axis):
    mx = jnp.max(val, axis=axis, keepdims=True)
    idx = jnp.min(jnp.where(val == mx, iota, n), axis=axis, keepdims=True)
    return mx, idx


def _router_kernel(x_ref, wr_ref, rb_ref, idx_ref, gate_ref, rank_ref, ps_ref, be_ref, bn_ref, carry):
    tm = x_ref.shape[0]
    neg = -jnp.inf

    @pl.when(pl.program_id(1) == 0)
    def _():
        carry[...] = jnp.zeros(carry.shape, F32)

    logits = lax.dot_general(wr_ref[...], x_ref[...], (((1,), (1,)), ((), ())),
                             precision=lax.Precision.HIGHEST, preferred_element_type=F32)
    scores = _sigmoid(logits)
    biased = scores + rb_ref[...]
    grp = biased.reshape(N_GROUPS, GROUP_SIZE, tm)
    io_g = lax.broadcasted_iota(jnp.int32, (N_GROUPS, GROUP_SIZE, tm), 1)
    m1, i1 = _first_argmax(grp, io_g, GROUP_SIZE, 1)
    m2 = jnp.max(jnp.where(io_g == i1, neg, grp), axis=1, keepdims=True)
    gscore = (m1 + m2).reshape(N_GROUPS, tm)
    io_n = lax.broadcasted_iota(jnp.int32, (N_GROUPS, tm), 0)
    gsel = jnp.zeros((N_GROUPS, tm), jnp.bool_)
    for _ in range(TOPK_GROUPS):
        _, gi = _first_argmax(gscore, io_n, N_GROUPS, 0)
        hit = io_n == gi
        gsel = jnp.logical_or(gsel, hit)
        gscore = jnp.where(hit, neg, gscore)
    emask = jnp.broadcast_to(gsel.reshape(N_GROUPS, 1, tm), (N_GROUPS, GROUP_SIZE, tm)).reshape(N_EXPERTS, tm)
    val = jnp.where(emask, biased, neg)
    io_e = lax.broadcasted_iota(jnp.int32, (N_EXPERTS, tm), 0)
    idxs, ws, hits = [], [], []
    for _ in range(TOP_K):
        _, ei = _first_argmax(val, io_e, N_EXPERTS, 0)
        hit = io_e == ei
        ws.append(jnp.sum(jnp.where(hit, scores, 0.0), axis=0, keepdims=True))
        idxs.append(ei)
        hits.append(hit)
        val = jnp.where(hit, neg, val)
    w = jnp.concatenate(ws, axis=0)
    idx_ref[...] = jnp.concatenate(idxs, axis=0)
    gate_ref[...] = w / jnp.sum(w, axis=0, keepdims=True) * ROUTED_SCALE

    onehot = jnp.zeros((N_EXPERTS, tm), F32)
    for hit in hits:
        onehot = onehot + hit.astype(F32)
    tri = (lax.broadcasted_iota(jnp.int32, (tm, tm), 0) <= lax.broadcasted_iota(jnp.int32, (tm, tm), 1))
    csum = _dot(onehot.astype(BF16), tri.astype(BF16)) + carry[...]
    ranks = [jnp.sum(jnp.where(hit, csum, 0.0), axis=0, keepdims=True) - 1.0 for hit in hits]
    rank_ref[...] = jnp.concatenate(ranks, axis=0).astype(jnp.int32)
    counts = csum[:, tm - 1:tm]
    carry[...] = counts

    @pl.when(pl.program_id(1) == pl.num_programs(1) - 1)
    def _():
        r = float(MOE_R)
        padded = jnp.floor((counts + (r - 1.0)) / r) * r
        below = (lax.broadcasted_iota(jnp.int32, (N_EXPERTS, N_EXPERTS), 1)
                 < lax.broadcasted_iota(jnp.int32, (N_EXPERTS, N_EXPERTS), 0)).astype(F32)
        pstart = jnp.dot(below, jnp.broadcast_to(padded, (N_EXPERTS, LANES)), precision=lax.Precision.HIGHEST,
                         preferred_element_type=F32)[:, 0:1]
        pend = pstart + padded
        nb = be_ref.shape[-1]
        bpos = lax.broadcasted_iota(jnp.int32, (1, nb), 1).astype(F32) * r
        be = jnp.minimum(jnp.sum((bpos >= pend).astype(F32), axis=0, keepdims=True), float(N_EXPERTS - 1))
        sel = lax.broadcasted_iota(jnp.int32, (N_EXPERTS, nb), 0) == be.astype(jnp.int32)
        cnt_b = jnp.sum(jnp.where(sel, counts, 0.0), axis=0, keepdims=True)
        st_b = jnp.sum(jnp.where(sel, pstart, 0.0), axis=0, keepdims=True)
        ps_ref[0] = pstart.astype(jnp.int32)
        be_ref[0] = be.astype(jnp.int32)
        bn_ref[0] = jnp.clip(cnt_b - (bpos - st_b), 0.0, r).astype(jnp.int32)


def _router(x2, wr_t, rb, nseg, nblk_pad):
    t, d = x2.shape
    tm = LANES
    tiles = t // (nseg * tm)
    tok_spec = pl.BlockSpec((TOP_K, tm), lambda s, j: (s * tiles + j, 0))
    tab_spec = pl.BlockSpec((1, 1, nblk_pad), lambda s, j: (s, 0, 0))
    tok_shape = (t // tm * TOP_K, tm)
    return pl.pallas_call(
        _router_kernel,
        grid=(nseg, tiles),
        in_specs=[pl.BlockSpec((tm, d), lambda s, j: (s * tiles + j, 0)), _full(wr_t.shape), _full(rb.shape)],
        out_specs=[tok_spec, tok_spec, tok_spec, pl.BlockSpec((1, N_EXPERTS, 1), lambda s, j: (s, 0, 0)),
                   tab_spec, tab_spec],
        out_shape=[jax.ShapeDtypeStruct(tok_shape, jnp.int32), jax.ShapeDtypeStruct(tok_shape, F32),
                   jax.ShapeDtypeStruct(tok_shape, jnp.int32),
                   jax.ShapeDtypeStruct((nseg, N_EXPERTS, 1), jnp.int32),
                   jax.ShapeDtypeStruct((nseg, 1, nblk_pad), jnp.int32),
                   jax.ShapeDtypeStruct((nseg, 1, nblk_pad), jnp.int32)],
        scratch_shapes=[pltpu.VMEM((N_EXPERTS, 1), F32)],
        compiler_params=_cp(("arbitrary", "arbitrary")),
        name="router",
    )(x2, wr_t, rb)


def _plan_sc(idx_f, rank_f, gate_f, pstart_f, nseg, tseg, nslots):
    words = tseg * TOP_K
    chunk = words // 4
    assert chunk * 4 == words and chunk % 16 == 0 and nslots % 16 == 0
    info = plsc.get_sparse_core_info()
    ncores = info.num_cores
    nl = info.num_lanes
    mesh = plsc.VectorSubcoreMesh(core_axis_name="c", subcore_axis_name="s")
    dummy_off = tseg * ROW_SLAB

    def body(idx_hbm, rank_hbm, gate_hbm, ps_hbm, *rest):
        outs = rest[:2 * nseg]
        idx_v, rank_v, val_v, ps_v, slot_v = rest[2 * nseg:]
        wid = lax.axis_index("s") * ncores + lax.axis_index("c")
        lane = lax.iota(jnp.int32, nl)

        def run(seg, want_gate):
            out_hbm = outs[2 * seg + (1 if want_gate else 0)]
            pltpu.sync_copy(ps_hbm.at[pl.ds(seg * N_EXPERTS, N_EXPERTS)], ps_v)
            fill = jnp.full((nl,), 0 if want_gate else dummy_off, jnp.int32)

            @pl.loop(0, nslots, step=nl)
            def _(i):
                slot_v[pl.ds(i, nl)] = fill

            @pl.loop(0, 4)
            def _(ci):
                base = seg * words + ci * chunk
                pltpu.sync_copy(idx_hbm.at[pl.ds(base, chunk)], idx_v)
                pltpu.sync_copy(rank_hbm.at[pl.ds(base, chunk)], rank_v)
                if want_gate:
                    pltpu.sync_copy(gate_hbm.at[pl.ds(base, chunk)], val_v)

                @pl.loop(0, chunk, step=nl)
                def _(o):
                    e = idx_v[pl.ds(o, nl)]
                    d = plsc.load_gather(ps_v, [e]) + rank_v[pl.ds(o, nl)]
                    if want_gate:
                        plsc.store_scatter(slot_v, [d], val_v[pl.ds(o, nl)])
                    else:
                        f = ci * chunk + o + lane
                        tok = ((f >> 10) << 7) | (f & (LANES - 1))
                        plsc.store_scatter(slot_v, [d], tok * ROW_SLAB)

            pltpu.sync_copy(slot_v, out_hbm)

        for seg in range(nseg):
            for want_gate in (False, True):
                @pl.when(wid == 2 * seg + (1 if want_gate else 0))
                def _(seg=seg, want_gate=want_gate):
                    run(seg, want_gate)

    out_type = [jax.ShapeDtypeStruct((nslots,), jnp.int32) for _ in range(2 * nseg)]
    fn = pl.kernel(
        body, out_type=out_type, mesh=mesh,
        scratch_types=[pltpu.VMEM((chunk,), jnp.int32), pltpu.VMEM((chunk,), jnp.int32),
                       pltpu.VMEM((chunk,), jnp.int32), pltpu.VMEM((N_EXPERTS,), jnp.int32),
                       pltpu.VMEM((nslots,), jnp.int32)],
        compiler_params=pltpu.CompilerParams(needs_layout_passes=False), name="plan_sc")
    return fn(idx_f, rank_f, lax.bitcast_convert_type(gate_f, jnp.int32), pstart_f)


def _expert_kernel(be_ref, bn_ref, slot_ref, gate_ref, x_ref, w13_ref, w2_ref, acc_ref, tile, ys, *, tseg):
    del be_ref
    b = pl.program_id(0)
    r = MOE_R
    s = TILE_S
    zero_rows = 64 * ROW_SLAB
    last_off = (tseg - 1) * ROW_SLAB

    @pl.when(b == 0)
    def _():
        def zero(i, c):
            acc_ref[pl.ds(pl.multiple_of(i * zero_rows, zero_rows), zero_rows), :] = jnp.zeros((zero_rows, LANES), F32)
            return c

        lax.fori_loop(0, acc_ref.shape[0] // zero_rows, zero, 0)
        rem = acc_ref.shape[0] % zero_rows
        if rem:
            acc_ref[pl.ds(acc_ref.shape[0] - rem, rem), :] = jnp.zeros((rem, LANES), F32)
        tile[...] = jnp.zeros(tile.shape, F32)

    n = bn_ref[b]

    @pl.when(n > 0)
    def _():
        base = b * r
        nchunk = (n + (SUBLANES - 1)) // SUBLANES

        def gather(q, carry):
            for j in range(SUBLANES):
                i = q * SUBLANES + j
                off = pl.multiple_of(jnp.minimum(slot_ref[base + i], last_off), ROW_SLAB)
                tile[pl.ds(i, ROW_SLAB, stride=s), :] = x_ref[pl.ds(off, ROW_SLAB), :]
            return carry

        lax.fori_loop(0, nchunk, gather, 0)
        xb = jnp.concatenate([tile[c * s:c * s + r, :] for c in range(ROW_SLAB)], axis=-1).astype(BF16)
        h13 = _dot(xb, w13_ref[0])
        hh = (_silu(h13[:, :D_EXPERT]) * h13[:, D_EXPERT:]).astype(BF16)
        y = _dot(hh, w2_ref[0])
        for g in range(r // SUBLANES):
            for c in range(ROW_SLAB):
                ys[pl.ds((g * ROW_SLAB + c) * SUBLANES, SUBLANES), :] = (
                    y[g * SUBLANES:(g + 1) * SUBLANES, c * LANES:(c + 1) * LANES])

        def scatter(q, carry):
            for half in range(2):
                offs, vals = [], []
                for j in range(half * 4, half * 4 + 4):
                    i = q * SUBLANES + j
                    off = pl.multiple_of(slot_ref[base + i], ROW_SLAB)
                    gte = gate_ref[base + i]
                    src = q * (ROW_SLAB * SUBLANES) + j
                    vals.append(acc_ref[pl.ds(off, ROW_SLAB), :]
                                + ys[pl.ds(src, ROW_SLAB, stride=SUBLANES), :] * gte)
                    offs.append(off)
                for off, val in zip(offs, vals):
                    acc_ref[pl.ds(off, ROW_SLAB), :] = val
            return carry

        lax.fori_loop(0, nchunk, scatter, 0)


def _expert_segment(blk_e, blk_n, slots, gates, xslab, w13, w2, tseg):
    nblk = -(-(tseg * TOP_K + N_EXPERTS * (MOE_R - 1)) // MOE_R)
    r = MOE_R
    acc_rows = (tseg + SUBLANES) * ROW_SLAB
    one = pl.Buffered(1)
    grid_spec = pltpu.PrefetchScalarGridSpec(
        num_scalar_prefetch=4,
        grid=(nblk,),
        in_specs=[pl.BlockSpec(xslab.shape, lambda b, *_: (0, 0), pipeline_mode=one),
                  pl.BlockSpec((1, D_MODEL, 2 * D_EXPERT), lambda b, be, *_: (be[b], 0, 0)),
                  pl.BlockSpec((1, D_EXPERT, D_MODEL), lambda b, be, *_: (be[b], 0, 0))],
        out_specs=pl.BlockSpec((acc_rows, LANES), lambda b, *_: (0, 0), pipeline_mode=one),
        scratch_shapes=[pltpu.VMEM((ROW_SLAB * TILE_S, LANES), F32),
                        pltpu.VMEM((r * ROW_SLAB, LANES), F32)],
    )
    return pl.pallas_call(
        functools.partial(_expert_kernel, tseg=tseg),
        grid_spec=grid_spec,
        out_shape=jax.ShapeDtypeStruct((acc_rows, LANES), F32),
        compiler_params=_cp(("arbitrary",)),
        name="experts",
    )(blk_e, blk_n, slots, gates, xslab, w13, w2)


def _final_kernel(x_ref, r_ref, w13_ref, w2_ref, g_ref, b_ref, o_ref):
    x = x_ref[...]
    h13 = _dot(x.astype(BF16), w13_ref[...])
    hh = (_silu(h13[:, :D_EXPERT]) * h13[:, D_EXPERT:]).astype(BF16)
    sh = _dot(hh, w2_ref[...])
    o_ref[...] = _ln(ALPHA * x + (sh + r_ref[...]), g_ref[...], b_ref[...])


def _final(x2, routed, w13, w2, g, b):
    t, d = x2.shape
    tm = TOK_TM
    return pl.pallas_call(
        _final_kernel,
        grid=(t // tm,),
        in_specs=[pl.BlockSpec((tm, d), lambda i: (i, 0)), pl.BlockSpec((tm, d), lambda i: (i, 0)),
                  _full(w13.shape), _full(w2.shape), _full(g.shape), _full(b.shape)],
        out_specs=pl.BlockSpec((tm, d), lambda i: (i, 0)),
        out_shape=jax.ShapeDtypeStruct((t, d), F32),
        compiler_params=_cp(("parallel",)),
        name="final",
    )(x2, routed, w13, w2, g, b)


def _block_diag_dense(w):
    h, dh, _ = w.shape
    eye = jnp.eye(h, dtype=w.dtype)
    return jnp.einsum('hij,hg->higj', w, eye).reshape(h * dh, h * dh)


def kernel(x_prompt, x_sample, mem_prompt, state_rglru_h, state_rglru_conv, state_dwconv, cache_mem_k, cache_mem_v, w_in, b_in, w_conv_a, b_conv_a, w_rg_a, b_rg_a, w_rg_x, b_rg_x, rg_lambda, w_conv_b, b_conv_b, ln_c_g, ln_c_b, w_out, ln1_g, ln1_b, w_q, w_k, w_v, w_o, ln2_g, ln2_b, w_router, router_bias, w1_e, w3_e, w2_e, w1_s, w3_s, w2_s, ln3_g, ln3_b):
    assert w_in.shape[0] == 1, "single-layer trunk"
    nb, L, d = x_prompt.shape
    ns = x_sample.shape[0]
    nm = mem_prompt.shape[1]
    row = lambda a: a.reshape(1, -1)

    wp = dict(
        w_in=w_in[0].astype(BF16), b_in=row(b_in[0]),
        w_conv_a=w_conv_a[0], b_conv_a=row(b_conv_a[0]),
        w_gate=jnp.concatenate([_block_diag_dense(w_rg_a[0]), _block_diag_dense(w_rg_x[0])], axis=1).astype(BF16),
        b_gate=row(jnp.concatenate([b_rg_a[0], b_rg_x[0]])),
        lam=row(rg_lambda[0]),
        w_conv_b=w_conv_b[0], b_conv_b=row(b_conv_b[0]),
        ln_c_g=row(ln_c_g[0]), ln_c_b=row(ln_c_b[0]),
        w_out=w_out[0].astype(BF16), ln1_g=row(ln1_g[0]), ln1_b=row(ln1_b[0]),
    )
    wq = w_q[0].astype(BF16)
    wo = w_o[0].astype(BF16)
    wkv = jnp.concatenate([w_k[0], w_v[0]], axis=1).astype(BF16)
    l2g, l2b = row(ln2_g[0]), row(ln2_b[0])

    x1_p, h_p, ca_p, cb_p = _mix_prompt(x_prompt, wp)
    kv = _linear(mem_prompt.reshape(nb * nm, d), wkv, 512, "kv_proj")
    mk_p = kv[:, :d].reshape(nb, nm, d)
    mv_p = kv[:, d:].reshape(nb, nm, d)
    x2_p = _attn_prompt(x1_p, mk_p, mv_p, wq, wo, l2g, l2b)

    xs = x_sample.reshape(ns, d)
    sta = jnp.transpose(state_rglru_conv[0], (1, 0, 2))
    stb = jnp.transpose(state_dwconv[0], (1, 0, 2))
    x1_s, h_s, u_s, glu_s = _mix_sample(xs, sta, stb, state_rglru_h[0], wp)
    q_s = _linear(x1_s, wq, ns, "q_sample")
    o_s = _attn_sample(q_s, cache_mem_k, cache_mem_v)
    x2_s = _linear_res_ln(o_s, x1_s, wo, l2g, l2b, ns, "attn_out_sample")

    x2 = jnp.concatenate([x2_p.reshape(nb * L, d), x2_s], axis=0)
    t = x2.shape[0]
    nseg = MOE_NSEG
    tseg = t // nseg
    assert tseg * nseg == t and tseg % LANES == 0 and t % TOK_TM == 0
    nblk = -(-(tseg * TOP_K + N_EXPERTS * (MOE_R - 1)) // MOE_R)
    nblk_pad = -(-nblk // LANES) * LANES
    idx_f, gate_f, rank_f, pstart, blk_e, blk_n = _router(
        x2, jnp.transpose(w_router[0]), router_bias[0].reshape(N_EXPERTS, 1), nseg, nblk_pad)
    plan = _plan_sc(idx_f.reshape(-1), rank_f.reshape(-1), gate_f.reshape(-1), pstart.reshape(-1),
                    nseg, tseg, nblk * MOE_R)
    w13_e = jnp.concatenate([w1_e[0], w3_e[0]], axis=-1).astype(BF16)
    w2_eb = w2_e[0].astype(BF16)
    xslab = x2.reshape(nseg, tseg * ROW_SLAB, LANES)
    parts = []
    for si in range(nseg):
        slots = plan[2 * si]
        gates = lax.bitcast_convert_type(plan[2 * si + 1], F32)
        acc = _expert_segment(blk_e[si, 0], blk_n[si, 0], slots, gates, xslab[si], w13_e, w2_eb, tseg)
        parts.append(acc[:tseg * ROW_SLAB].reshape(tseg, d))
    routed = jnp.concatenate(parts, axis=0)
    w13_s = jnp.concatenate([w1_s[0], w3_s[0]], axis=-1).astype(BF16)
    y = _final(x2, routed, w13_s, w2_s[0].astype(BF16), row(ln3_g[0]), row(ln3_b[0]))

    y_p = y[:nb * L].reshape(nb, L, d)
    y_s = y[nb * L:].reshape(ns, 1, d)
    new_ca_s = jnp.concatenate([state_rglru_conv[0][:, 1:], u_s[:, None, :]], axis=1)
    new_cb_s = jnp.concatenate([state_dwconv[0][:, 1:], glu_s[:, None, :]], axis=1)
    hd = XA_HEAD_DIM
    return (y_p, y_s, h_p[None], ca_p[None], cb_p[None],
            mk_p.reshape(1, nb, nm, XA_HEADS, hd), mv_p.reshape(1, nb, nm, XA_HEADS, hd),
            h_s[None], new_ca_s[None], new_cb_s[None])
```

```python
import functools

import jax
import jax.numpy as jnp
from jax import lax
from jax.experimental import pallas as pl
from jax.experimental.pallas import tpu as pltpu
from jax.experimental.pallas import tpu_sc as plsc

F32 = jnp.float32
BF16 = jnp.bfloat16

D_MODEL = 1024
D_REC = 512
D_CONV = 512
REC_HEADS = 8
REC_HEAD_DIM = D_REC // REC_HEADS
REC_CONV_W = 4
RG_C = 8.0
CONF_CONV_W = 31
XA_HEADS = 4
XA_HEAD_DIM = D_MODEL // XA_HEADS
N_EXPERTS = 64
TOP_K = 8
N_GROUPS = 8
TOPK_GROUPS = 4
GROUP_SIZE = N_EXPERTS // N_GROUPS
D_EXPERT = D_MODEL // 4
ROUTED_SCALE = 2.5
LN_EPS = 1e-5
DEPTH = 1
ALPHA = (2 * DEPTH) ** 0.25

LANES = 128
SUBLANES = 8
VMEM_LIMIT = 56 * 1024 * 1024

MIX_TL = 64
SCAN_PAD = 8
ATT_TM = 512
ATT_G = 4
TOK_TM = 384
MOE_NSEG = 3
MOE_R = 128
ROW_SLAB = D_MODEL // LANES
TILE_S = MOE_R + 8


def _cp(sem=None):
    return pltpu.CompilerParams(dimension_semantics=sem, vmem_limit_bytes=VMEM_LIMIT)


def _ln(x, g, b):
    mu = jnp.mean(x, axis=-1, keepdims=True)
    xc = x - mu
    var = jnp.mean(xc * xc, axis=-1, keepdims=True)
    return xc * lax.rsqrt(var + LN_EPS) * g + b


def _sigmoid(x):
    return 1.0 / (1.0 + jnp.exp(-x))


def _silu(x):
    return x * _sigmoid(x)


def _softplus(x):
    return jnp.maximum(x, 0.0) + jnp.log1p(jnp.exp(-jnp.abs(x)))


def _gelu_tanh(x):
    c = 0.7978845608028654
    return 0.5 * x * (1.0 + jnp.tanh(c * (x + 0.044715 * (x * x * x))))


def _dot(a, b):
    return jnp.dot(a, b, preferred_element_type=F32)


def _full(shape):
    n = len(shape)
    return pl.BlockSpec(shape, lambda *_: (0,) * n)


def _rglru_coeffs(ucf, wg_ref, bg_ref, lam_ref):
    gates = _dot(ucf.astype(BF16), wg_ref[...]) + bg_ref[...]
    r = _sigmoid(gates[:, :D_REC])
    ig = _sigmoid(gates[:, D_REC:])
    log_a = (-RG_C) * r * _softplus(-lam_ref[...])
    a = jnp.exp(log_a)
    bt = jnp.sqrt(-jnp.tanh(log_a) * (a * a + 1.0)) * (ig * ucf)
    return a, bt


def _mix_prompt_kernel(x_ref, win_ref, bin_ref, wca_ref, bca_ref, wg_ref, bg_ref, lam_ref,
                       wcb_ref, bcb_ref, lcg_ref, lcb_ref, wout_ref, l1g_ref, l1b_ref,
                       x1_ref, hl_ref, ca_ref, cb_ref,
                       exta, extb, a_s, b_s, h_s, hc, *, nb, tl):
    i = pl.program_id(0)
    m = nb * tl
    tlp = tl + SCAN_PAD
    pa = SUBLANES
    pb = 4 * SUBLANES
    wa = REC_CONV_W
    wb = CONF_CONV_W

    @pl.when(i == 0)
    def _():
        exta[:, 0:pa, :] = jnp.zeros((nb, pa, D_REC), F32)
        extb[:, 0:pb, :] = jnp.zeros((nb, pb, D_CONV), F32)
        hc[...] = jnp.zeros((nb, D_REC), F32)

    xf = x_ref[...].reshape(m, D_MODEL)
    z = _dot(xf.astype(BF16), win_ref[...]) + bin_ref[...]
    g = z[:, 0:D_REC]
    u = z[:, D_REC:2 * D_REC]
    v = z[:, 2 * D_REC:2 * D_REC + D_CONV]
    gg = z[:, 2 * D_REC + D_CONV:]

    exta[:, pa:pa + tl, :] = u.reshape(nb, tl, D_REC)
    uc = jnp.zeros((nb, tl, D_REC), F32) + bca_ref[...]
    for j in range(wa):
        s0 = pa - (wa - 1) + j
        uc = uc + wca_ref[j:j + 1, :] * exta[:, s0:s0 + tl, :]
    tail_a = exta[:, pa + tl - (wa - 1):pa + tl, :]
    ca_ref[...] = tail_a
    exta[:, pa - (wa - 1):pa, :] = tail_a

    ucf = uc.reshape(m, D_REC)
    a, bt = _rglru_coeffs(ucf, wg_ref, bg_ref, lam_ref)

    nlt = D_REC // LANES
    for n in range(nb):
        for c in range(nlt):
            a_s[c, pl.ds(n * tlp, tl), :] = a[n * tl:(n + 1) * tl, c * LANES:(c + 1) * LANES]
            b_s[c, pl.ds(n * tlp, tl), :] = bt[n * tl:(n + 1) * tl, c * LANES:(c + 1) * LANES]

    def step(t, hs):
        out = []
        for c in range(nlt):
            hn = a_s[c, pl.ds(t, nb, stride=tlp), :] * hs[c] + b_s[c, pl.ds(t, nb, stride=tlp), :]
            h_s[c, pl.ds(t, nb, stride=tlp), :] = hn
            out.append(hn)
        return tuple(out)

    hs = lax.fori_loop(0, tl, step, tuple(hc[:, c * LANES:(c + 1) * LANES] for c in range(nlt)))
    h = jnp.concatenate(hs, axis=-1)
    hc[...] = h
    hl_ref[...] = h
    hseq = jnp.concatenate(
        [jnp.concatenate([h_s[c, pl.ds(n * tlp, tl), :] for c in range(nlt)], axis=-1) for n in range(nb)],
        axis=0)
    y_a = _gelu_tanh(g) * hseq

    glu = v * _sigmoid(gg)
    extb[:, pb:pb + tl, :] = glu.reshape(nb, tl, D_CONV)
    c = jnp.zeros((nb, tl, D_CONV), F32) + bcb_ref[...]
    for j in range(wb):
        s0 = pb - (wb - 1) + j
        c = c + wcb_ref[j:j + 1, :] * extb[:, s0:s0 + tl, :]
    tail_b = extb[:, pb + tl - (wb - 1):pb + tl, :]
    cb_ref[...] = tail_b
    extb[:, pb - (wb - 1):pb, :] = tail_b
    y_b = _silu(_ln(c.reshape(m, D_CONV), lcg_ref[...], lcb_ref[...]))

    mix = jnp.concatenate([y_a, y_b], axis=-1).astype(BF16)
    o = _dot(mix, wout_ref[...])
    x1 = _ln(ALPHA * xf + o, l1g_ref[...], l1b_ref[...])
    x1_ref[...] = x1.reshape(nb, tl, D_MODEL)


def _mix_prompt(x, wp):
    nb, L, _ = x.shape
    tl = MIX_TL
    tlp = tl + SCAN_PAD
    kern = functools.partial(_mix_prompt_kernel, nb=nb, tl=tl)
    ins = [wp['w_in'], wp['b_in'], wp['w_conv_a'], wp['b_conv_a'], wp['w_gate'], wp['b_gate'],
           wp['lam'], wp['w_conv_b'], wp['b_conv_b'], wp['ln_c_g'], wp['ln_c_b'], wp['w_out'],
           wp['ln1_g'], wp['ln1_b']]
    return pl.pallas_call(
        kern,
        grid=(L // tl,),
        in_specs=[pl.BlockSpec((nb, tl, D_MODEL), lambda i: (0, i, 0))] + [_full(a.shape) for a in ins],
        out_specs=[pl.BlockSpec((nb, tl, D_MODEL), lambda i: (0, i, 0)),
                   _full((nb, D_REC)), _full((nb, REC_CONV_W - 1, D_REC)),
                   _full((nb, CONF_CONV_W - 1, D_CONV))],
        out_shape=[jax.ShapeDtypeStruct((nb, L, D_MODEL), F32),
                   jax.ShapeDtypeStruct((nb, D_REC), F32),
                   jax.ShapeDtypeStruct((nb, REC_CONV_W - 1, D_REC), F32),
                   jax.ShapeDtypeStruct((nb, CONF_CONV_W - 1, D_CONV), F32)],
        scratch_shapes=[pltpu.VMEM((nb, SUBLANES + tl, D_REC), F32),
                        pltpu.VMEM((nb, 4 * SUBLANES + tl, D_CONV), F32),
                        pltpu.VMEM((D_REC // LANES, nb * tlp, LANES), F32),
                        pltpu.VMEM((D_REC // LANES, nb * tlp, LANES), F32),
                        pltpu.VMEM((D_REC // LANES, nb * tlp, LANES), F32),
                        pltpu.VMEM((nb, D_REC), F32)],
        compiler_params=_cp(("arbitrary",)),
        name="mix_prompt",
    )(x, *ins)


def _mix_sample_kernel(x_ref, sta_ref, stb_ref, h0_ref, win_ref, bin_ref, wca_ref, bca_ref, wg_ref,
                       bg_ref, lam_ref, wcb_ref, bcb_ref, lcg_ref, lcb_ref, wout_ref, l1g_ref, l1b_ref,
                       x1_ref, h_ref, u_ref, glu_ref):
    wa = REC_CONV_W
    wb = CONF_CONV_W
    xf = x_ref[...]
    z = _dot(xf.astype(BF16), win_ref[...]) + bin_ref[...]
    g = z[:, 0:D_REC]
    u = z[:, D_REC:2 * D_REC]
    v = z[:, 2 * D_REC:2 * D_REC + D_CONV]
    gg = z[:, 2 * D_REC + D_CONV:]

    uc = bca_ref[...] + wca_ref[wa - 1:wa, :] * u
    for j in range(wa - 1):
        uc = uc + wca_ref[j:j + 1, :] * sta_ref[j]
    a, bt = _rglru_coeffs(uc, wg_ref, bg_ref, lam_ref)
    h = a * h0_ref[...] + bt
    y_a = _gelu_tanh(g) * h

    glu = v * _sigmoid(gg)
    c = bcb_ref[...] + wcb_ref[wb - 1:wb, :] * glu
    for j in range(wb - 1):
        c = c + wcb_ref[j:j + 1, :] * stb_ref[j]
    y_b = _silu(_ln(c, lcg_ref[...], lcb_ref[...]))

    mix = jnp.concatenate([y_a, y_b], axis=-1).astype(BF16)
    o = _dot(mix, wout_ref[...])
    x1_ref[...] = _ln(ALPHA * xf + o, l1g_ref[...], l1b_ref[...])
    h_ref[...] = h
    u_ref[...] = u
    glu_ref[...] = glu


def _mix_sample(x, sta, stb, h0, wp):
    n = x.shape[0]
    ins = [x, sta, stb, h0, wp['w_in'], wp['b_in'], wp['w_conv_a'], wp['b_conv_a'], wp['w_gate'],
           wp['b_gate'], wp['lam'], wp['w_conv_b'], wp['b_conv_b'], wp['ln_c_g'], wp['ln_c_b'],
           wp['w_out'], wp['ln1_g'], wp['ln1_b']]
    return pl.pallas_call(
        _mix_sample_kernel,
        grid=(1,),
        in_specs=[_full(a.shape) for a in ins],
        out_specs=[_full((n, D_MODEL)), _full((n, D_REC)), _full((n, D_REC)), _full((n, D_CONV))],
        out_shape=[jax.ShapeDtypeStruct((n, D_MODEL), F32), jax.ShapeDtypeStruct((n, D_REC), F32),
                   jax.ShapeDtypeStruct((n, D_REC), F32), jax.ShapeDtypeStruct((n, D_CONV), F32)],
        compiler_params=_cp(("arbitrary",)),
        name="mix_sample",
    )(*ins)


def _linear_kernel(a_ref, w_ref, o_ref):
    o_ref[...] = _dot(a_ref[...].astype(BF16), w_ref[...])


def _linear(a, w, tm, name):
    m, k = a.shape
    n = w.shape[1]
    return pl.pallas_call(
        _linear_kernel,
        grid=(m // tm,),
        in_specs=[pl.BlockSpec((tm, k), lambda i: (i, 0)), _full(w.shape)],
        out_specs=pl.BlockSpec((tm, n), lambda i: (i, 0)),
        out_shape=jax.ShapeDtypeStruct((m, n), F32),
        compiler_params=_cp(("parallel",)),
        name=name,
    )(a, w)


def _linear_res_ln_kernel(a_ref, r_ref, w_ref, g_ref, b_ref, o_ref):
    o = _dot(a_ref[...].astype(BF16), w_ref[...])
    o_ref[...] = _ln(ALPHA * r_ref[...] + o, g_ref[...], b_ref[...])


def _linear_res_ln(a, res, w, g, b, tm, name):
    m, k = a.shape
    n = w.shape[1]
    return pl.pallas_call(
        _linear_res_ln_kernel,
        grid=(m // tm,),
        in_specs=[pl.BlockSpec((tm, k), lambda i: (i, 0)), pl.BlockSpec((tm, n), lambda i: (i, 0)),
                  _full(w.shape), _full(g.shape), _full(b.shape)],
        out_specs=pl.BlockSpec((tm, n), lambda i: (i, 0)),
        out_shape=jax.ShapeDtypeStruct((m, n), F32),
        compiler_params=_cp(("parallel",)),
        name=name,
    )(a, res, w, g, b)


def _attn_prompt_kernel(x_ref, k_ref, v_ref, wq_ref, wo_ref, g_ref, b_ref, o_ref):
    x = x_ref[0]
    q = _dot(x.astype(BF16), wq_ref[...])
    k = k_ref[0].astype(BF16)
    v = v_ref[0].astype(BF16)
    scale = XA_HEAD_DIM ** -0.5
    outs = []
    for h in range(XA_HEADS):
        sl = slice(h * XA_HEAD_DIM, (h + 1) * XA_HEAD_DIM)
        s = lax.dot_general(q[:, sl].astype(BF16), k[:, sl], (((1,), (1,)), ((), ())),
                            preferred_element_type=F32) * scale
        e = jnp.exp(s - jnp.max(s, axis=-1, keepdims=True))
        p = e / jnp.sum(e, axis=-1, keepdims=True)
        outs.append(_dot(p.astype(BF16), v[:, sl]))
    o = jnp.concatenate(outs, axis=-1).astype(BF16)
    att = _dot(o, wo_ref[...])
    o_ref[0] = _ln(ALPHA * x + att, g_ref[...], b_ref[...])


def _attn_prompt(x1, mk, mv, wq, wo, g, b):
    nb, L, d = x1.shape
    nm = mk.shape[1]
    tm = ATT_TM
    return pl.pallas_call(
        _attn_prompt_kernel,
        grid=(nb, L // tm),
        in_specs=[pl.BlockSpec((1, tm, d), lambda bi, i: (bi, i, 0)),
                  pl.BlockSpec((1, nm, d), lambda bi, i: (bi, 0, 0)),
                  pl.BlockSpec((1, nm, d), lambda bi, i: (bi, 0, 0)),
                  _full(wq.shape), _full(wo.shape), _full(g.shape), _full(b.shape)],
        out_specs=pl.BlockSpec((1, tm, d), lambda bi, i: (bi, i, 0)),
        out_shape=jax.ShapeDtypeStruct((nb, L, d), F32),
        compiler_params=_cp(("parallel", "parallel")),
        name="attn_prompt",
    )(x1, mk, mv, wq, wo, g, b)


def _attn_sample_kernel(q_ref, k_ref, v_ref, o_ref):
    scale = XA_HEAD_DIM ** -0.5
    for gi in range(ATT_G):
        qn = q_ref[gi]
        s = jnp.sum(k_ref[gi] * qn[None], axis=-1, keepdims=True) * scale
        e = jnp.exp(s - jnp.max(s, axis=0, keepdims=True))
        p = e / jnp.sum(e, axis=0, keepdims=True)
        o_ref[gi] = jnp.sum(p * v_ref[gi], axis=0)


def _attn_sample(q, ck, cv):
    _, n, nm, nh, dh = ck.shape
    q3 = q.reshape(n, nh, dh)
    cache_spec = pl.BlockSpec((None, ATT_G, nm, nh, dh), lambda i: (0, i, 0, 0, 0))
    out = pl.pallas_call(
        _attn_sample_kernel,
        grid=(n // ATT_G,),
        in_specs=[pl.BlockSpec((ATT_G, nh, dh), lambda i: (i, 0, 0)), cache_spec, cache_spec],
        out_specs=pl.BlockSpec((ATT_G, nh, dh), lambda i: (i, 0, 0)),
        out_shape=jax.ShapeDtypeStruct((n, nh, dh), F32),
        compiler_params=_cp(("parallel",)),
        name="attn_sample",
    )(q3, ck, cv)
    return out.reshape(n, nh * dh)


def _first_argmax(val, iota, n, axis):
    mx = jnp.max(val, axis=axis, keepdims=True)
    idx = jnp.min(jnp.where(val == mx, iota, n), axis=axis, keepdims=True)
    return mx, idx


def _router_kernel(x_ref, wr_ref, rb_ref, idx_ref, gate_ref, rank_ref, ps_ref, be_ref, bn_ref, carry):
    tm = x_ref.shape[0]
    neg = -jnp.inf

    @pl.when(pl.program_id(1) == 0)
    def _():
        carry[...] = jnp.zeros(carry.shape, F32)

    logits = lax.dot_general(wr_ref[...], x_ref[...], (((1,), (1,)), ((), ())),
                             precision=lax.Precision.HIGHEST, preferred_element_type=F32)
    scores = _sigmoid(logits)
    biased = scores + rb_ref[...]
    grp = biased.reshape(N_GROUPS, GROUP_SIZE, tm)
    io_g = lax.broadcasted_iota(jnp.int32, (N_GROUPS, GROUP_SIZE, tm), 1)
    m1, i1 = _first_argmax(grp, io_g, GROUP_SIZE, 1)
    m2 = jnp.max(jnp.where(io_g == i1, neg, grp), axis=1, keepdims=True)
    gscore = (m1 + m2).reshape(N_GROUPS, tm)
    io_n = lax.broadcasted_iota(jnp.int32, (N_GROUPS, tm), 0)
    gsel = jnp.zeros((N_GROUPS, tm), jnp.bool_)
    for _ in range(TOPK_GROUPS):
        _, gi = _first_argmax(gscore, io_n, N_GROUPS, 0)
        hit = io_n == gi
        gsel = jnp.logical_or(gsel, hit)
        gscore = jnp.where(hit, neg, gscore)
    emask = jnp.broadcast_to(gsel.reshape(N_GROUPS, 1, tm), (N_GROUPS, GROUP_SIZE, tm)).reshape(N_EXPERTS, tm)
    val = jnp.where(emask, biased, neg)
    io_e = lax.broadcasted_iota(jnp.int32, (N_EXPERTS, tm), 0)
    idxs, ws, hits = [], [], []
    for _ in range(TOP_K):
        _, ei = _first_argmax(val, io_e, N_EXPERTS, 0)
        hit = io_e == ei
        ws.append(jnp.sum(jnp.where(hit, scores, 0.0), axis=0, keepdims=True))
        idxs.append(ei)
        hits.append(hit)
        val = jnp.where(hit, neg, val)
    w = jnp.concatenate(ws, axis=0)
    idx_ref[...] = jnp.concatenate(idxs, axis=0)
    gate_ref[...] = w / jnp.sum(w, axis=0, keepdims=True) * ROUTED_SCALE

    onehot = jnp.zeros((N_EXPERTS, tm), F32)
    for hit in hits:
        onehot = onehot + hit.astype(F32)
    tri = (lax.broadcasted_iota(jnp.int32, (tm, tm), 0) <= lax.broadcasted_iota(jnp.int32, (tm, tm), 1))
    csum = _dot(onehot.astype(BF16), tri.astype(BF16)) + carry[...]
    ranks = [jnp.sum(jnp.where(hit, csum, 0.0), axis=0, keepdims=True) - 1.0 for hit in hits]
    rank_ref[...] = jnp.concatenate(ranks, axis=0).astype(jnp.int32)
    counts = csum[:, tm - 1:tm]
    carry[...] = counts

    @pl.when(pl.program_id(1) == pl.num_programs(1) - 1)
    def _():
        r = float(MOE_R)
        padded = jnp.floor((counts + (r - 1.0)) / r) * r
        below = (lax.broadcasted_iota(jnp.int32, (N_EXPERTS, N_EXPERTS), 1)
                 < lax.broadcasted_iota(jnp.int32, (N_EXPERTS, N_EXPERTS), 0)).astype(F32)
        pstart = jnp.dot(below, jnp.broadcast_to(padded, (N_EXPERTS, LANES)), precision=lax.Precision.HIGHEST,
                         preferred_element_type=F32)[:, 0:1]
        pend = pstart + padded
        nb = be_ref.shape[-1]
        bpos = lax.broadcasted_iota(jnp.int32, (1, nb), 1).astype(F32) * r
        be = jnp.minimum(jnp.sum((bpos >= pend).astype(F32), axis=0, keepdims=True), float(N_EXPERTS - 1))
        sel = lax.broadcasted_iota(jnp.int32, (N_EXPERTS, nb), 0) == be.astype(jnp.int32)
        cnt_b = jnp.sum(jnp.where(sel, counts, 0.0), axis=0, keepdims=True)
        st_b = jnp.sum(jnp.where(sel, pstart, 0.0), axis=0, keepdims=True)
        ps_ref[0] = pstart.astype(jnp.int32)
        be_ref[0] = be.astype(jnp.int32)
        bn_ref[0] = jnp.clip(cnt_b - (bpos - st_b), 0.0, r).astype(jnp.int32)


def _router(x2, wr_t, rb, nseg, nblk_pad):
    t, d = x2.shape
    tm = LANES
    tiles = t // (nseg * tm)
    tok_spec = pl.BlockSpec((TOP_K, tm), lambda s, j: (s * tiles + j, 0))
    tab_spec = pl.BlockSpec((1, 1, nblk_pad), lambda s, j: (s, 0, 0))
    tok_shape = (t // tm * TOP_K, tm)
    return pl.pallas_call(
        _router_kernel,
        grid=(nseg, tiles),
        in_specs=[pl.BlockSpec((tm, d), lambda s, j: (s * tiles + j, 0)), _full(wr_t.shape), _full(rb.shape)],
        out_specs=[tok_spec, tok_spec, tok_spec, pl.BlockSpec((1, N_EXPERTS, 1), lambda s, j: (s, 0, 0)),
                   tab_spec, tab_spec],
        out_shape=[jax.ShapeDtypeStruct(tok_shape, jnp.int32), jax.ShapeDtypeStruct(tok_shape, F32),
                   jax.ShapeDtypeStruct(tok_shape, jnp.int32),
                   jax.ShapeDtypeStruct((nseg, N_EXPERTS, 1), jnp.int32),
                   jax.ShapeDtypeStruct((nseg, 1, nblk_pad), jnp.int32),
                   jax.ShapeDtypeStruct((nseg, 1, nblk_pad), jnp.int32)],
        scratch_shapes=[pltpu.VMEM((N_EXPERTS, 1), F32)],
        compiler_params=_cp(("arbitrary", "arbitrary")),
        name="router",
    )(x2, wr_t, rb)


def _plan_sc(idx_f, rank_f, gate_f, pstart_f, nseg, tseg, nslots, lead):
    words = tseg * TOP_K
    chunk = words // 4
    assert chunk * 4 == words and chunk % 16 == 0 and nslots % 16 == 0
    info = plsc.get_sparse_core_info()
    ncores = info.num_cores
    nl = info.num_lanes
    mesh = plsc.VectorSubcoreMesh(core_axis_name="c", subcore_axis_name="s")
    dummy_off = tseg * ROW_SLAB

    def body(idx_hbm, rank_hbm, gate_hbm, ps_hbm, *rest):
        outs = rest[:2 * nseg]
        idx_v, rank_v, val_v, ps_v, slot_v = rest[2 * nseg:]
        wid = lax.axis_index("s") * ncores + lax.axis_index("c")
        lane = lax.iota(jnp.int32, nl)

        def run(seg, want_gate):
            out_hbm = outs[2 * seg + (1 if want_gate else 0)]
            pltpu.sync_copy(ps_hbm.at[pl.ds(seg * N_EXPERTS, N_EXPERTS)], ps_v)
            fill = jnp.full((nl,), 0 if want_gate else dummy_off, jnp.int32)

            @pl.loop(0, nslots, step=nl)
            def _(i):
                slot_v[pl.ds(i, nl)] = fill

            @pl.loop(0, 4)
            def _(ci):
                base = seg * words + ci * chunk
                pltpu.sync_copy(idx_hbm.at[pl.ds(base, chunk)], idx_v)
                pltpu.sync_copy(rank_hbm.at[pl.ds(base, chunk)], rank_v)
                if want_gate:
                    pltpu.sync_copy(gate_hbm.at[pl.ds(base, chunk)], val_v)

                @pl.loop(0, chunk, step=nl)
                def _(o):
                    e = idx_v[pl.ds(o, nl)]
                    d = plsc.load_gather(ps_v, [e]) + rank_v[pl.ds(o, nl)] + lead
                    if want_gate:
                        plsc.store_scatter(slot_v, [d], val_v[pl.ds(o, nl)])
                    else:
                        f = ci * chunk + o + lane
                        tok = ((f >> 10) << 7) | (f & (LANES - 1))
                        plsc.store_scatter(slot_v, [d], tok * ROW_SLAB)

            pltpu.sync_copy(slot_v, out_hbm)

        for seg in range(nseg):
            for want_gate in (False, True):
                @pl.when(wid == 2 * seg + (1 if want_gate else 0))
                def _(seg=seg, want_gate=want_gate):
                    run(seg, want_gate)

    out_type = [jax.ShapeDtypeStruct((nslots,), jnp.int32) for _ in range(2 * nseg)]
    fn = pl.kernel(
        body, out_type=out_type, mesh=mesh,
        scratch_types=[pltpu.VMEM((chunk,), jnp.int32), pltpu.VMEM((chunk,), jnp.int32),
                       pltpu.VMEM((chunk,), jnp.int32), pltpu.VMEM((N_EXPERTS,), jnp.int32),
                       pltpu.VMEM((nslots,), jnp.int32)],
        compiler_params=pltpu.CompilerParams(needs_layout_passes=False), name="plan_sc")
    return fn(idx_f, rank_f, lax.bitcast_convert_type(gate_f, jnp.int32), pstart_f)


def _expert_kernel(be_ref, bn_ref, slot_ref, gate_ref, x_ref, w13e_ref, w2e_ref, w13o_ref, w2o_ref,
                   acc_ref, tile_a, tile_b, ys_a, ys_b, *, tseg):
    del be_ref
    i = pl.program_id(0)
    r = MOE_R
    s = TILE_S
    zero_rows = 64 * ROW_SLAB
    last_off = (tseg - 1) * ROW_SLAB

    def gather(blk, tile):
        base = (blk + 1) * r
        for row in range(r):
            off = pl.multiple_of(jnp.minimum(slot_ref[base + row], last_off), ROW_SLAB)
            tile[pl.ds(row, ROW_SLAB, stride=s), :] = x_ref[pl.ds(off, ROW_SLAB), :]

    def ffn(tile, w13_ref, w2_ref, ys):
        xb = jnp.concatenate([tile[c * s:c * s + r, :] for c in range(ROW_SLAB)], axis=-1).astype(BF16)
        h13 = _dot(xb, w13_ref[0])
        hh = (_silu(h13[:, :D_EXPERT]) * h13[:, D_EXPERT:]).astype(BF16)
        y = _dot(hh, w2_ref[0])
        for g in range(r // SUBLANES):
            for c in range(ROW_SLAB):
                ys[pl.ds((g * ROW_SLAB + c) * SUBLANES, SUBLANES), :] = (
                    y[g * SUBLANES:(g + 1) * SUBLANES, c * LANES:(c + 1) * LANES])

    def scatter(blk, ys):
        base = (blk + 1) * r
        for q in range(r // 4):
            offs, vals = [], []
            for row in range(q * 4, q * 4 + 4):
                off = pl.multiple_of(slot_ref[base + row], ROW_SLAB)
                src = (row // SUBLANES) * (ROW_SLAB * SUBLANES) + row % SUBLANES
                vals.append(acc_ref[pl.ds(off, ROW_SLAB), :]
                            + ys[pl.ds(src, ROW_SLAB, stride=SUBLANES), :] * gate_ref[base + row])
                offs.append(off)
            for off, val in zip(offs, vals):
                acc_ref[pl.ds(off, ROW_SLAB), :] = val

    @pl.when(i == 0)
    def _():
        def zero(k, c):
            acc_ref[pl.ds(pl.multiple_of(k * zero_rows, zero_rows), zero_rows), :] = jnp.zeros((zero_rows, LANES), F32)
            return c

        lax.fori_loop(0, acc_ref.shape[0] // zero_rows, zero, 0)
        rem = acc_ref.shape[0] % zero_rows
        if rem:
            acc_ref[pl.ds(acc_ref.shape[0] - rem, rem), :] = jnp.zeros((rem, LANES), F32)
        tile_a[...] = jnp.zeros(tile_a.shape, F32)
        tile_b[...] = jnp.zeros(tile_b.shape, F32)
        ys_b[...] = jnp.zeros(ys_b.shape, F32)
        gather(0, tile_a)

    even = 2 * i
    odd = even + 1
    pending = bn_ref[even] + bn_ref[odd] + bn_ref[jnp.maximum(even - 1, 0)]

    @pl.when(pending > 0)
    def _():
        gather(odd, tile_b)
        ffn(tile_a, w13e_ref, w2e_ref, ys_a)
        scatter(even - 1, ys_b)
        gather(even + 2, tile_a)
        ffn(tile_b, w13o_ref, w2o_ref, ys_b)
        scatter(even, ys_a)


def _expert_segment(blk_e, blk_n, slots, gates, xslab, w13, w2, tseg, nblk):
    r = MOE_R
    acc_rows = (tseg + SUBLANES) * ROW_SLAB
    one = pl.Buffered(1)
    w13_blk = (1, D_MODEL, 2 * D_EXPERT)
    w2_blk = (1, D_EXPERT, D_MODEL)
    grid_spec = pltpu.PrefetchScalarGridSpec(
        num_scalar_prefetch=4,
        grid=(nblk // 2 + 1,),
        in_specs=[pl.BlockSpec(xslab.shape, lambda i, *_: (0, 0), pipeline_mode=one),
                  pl.BlockSpec(w13_blk, lambda i, be, *_: (be[2 * i], 0, 0)),
                  pl.BlockSpec(w2_blk, lambda i, be, *_: (be[2 * i], 0, 0)),
                  pl.BlockSpec(w13_blk, lambda i, be, *_: (be[2 * i + 1], 0, 0)),
                  pl.BlockSpec(w2_blk, lambda i, be, *_: (be[2 * i + 1], 0, 0))],
        out_specs=pl.BlockSpec((acc_rows, LANES), lambda i, *_: (0, 0), pipeline_mode=one),
        scratch_shapes=[pltpu.VMEM((ROW_SLAB * TILE_S, LANES), F32),
                        pltpu.VMEM((ROW_SLAB * TILE_S, LANES), F32),
                        pltpu.VMEM((r * ROW_SLAB, LANES), F32),
                        pltpu.VMEM((r * ROW_SLAB, LANES), F32)],
    )
    return pl.pallas_call(
        functools.partial(_expert_kernel, tseg=tseg),
        grid_spec=grid_spec,
        out_shape=jax.ShapeDtypeStruct((acc_rows, LANES), F32),
        compiler_params=_cp(("arbitrary",)),
        name="experts",
    )(blk_e, blk_n, slots, gates, xslab, w13, w2, w13, w2)


def _final_kernel(x_ref, r_ref, w13_ref, w2_ref, g_ref, b_ref, o_ref):
    x = x_ref[...]
    h13 = _dot(x.astype(BF16), w13_ref[...])
    hh = (_silu(h13[:, :D_EXPERT]) * h13[:, D_EXPERT:]).astype(BF16)
    sh = _dot(hh, w2_ref[...])
    o_ref[...] = _ln(ALPHA * x + (sh + r_ref[...]), g_ref[...], b_ref[...])


def _final(x2, routed, w13, w2, g, b):
    t, d = x2.shape
    tm = TOK_TM
    return pl.pallas_call(
        _final_kernel,
        grid=(t // tm,),
        in_specs=[pl.BlockSpec((tm, d), lambda i: (i, 0)), pl.BlockSpec((tm, d), lambda i: (i, 0)),
                  _full(w13.shape), _full(w2.shape), _full(g.shape), _full(b.shape)],
        out_specs=pl.BlockSpec((tm, d), lambda i: (i, 0)),
        out_shape=jax.ShapeDtypeStruct((t, d), F32),
        compiler_params=_cp(("parallel",)),
        name="final",
    )(x2, routed, w13, w2, g, b)


def _block_diag_dense(w):
    h, dh, _ = w.shape
    eye = jnp.eye(h, dtype=w.dtype)
    return jnp.einsum('hij,hg->higj', w, eye).reshape(h * dh, h * dh)


def kernel(x_prompt, x_sample, mem_prompt, state_rglru_h, state_rglru_conv, state_dwconv, cache_mem_k, cache_mem_v, w_in, b_in, w_conv_a, b_conv_a, w_rg_a, b_rg_a, w_rg_x, b_rg_x, rg_lambda, w_conv_b, b_conv_b, ln_c_g, ln_c_b, w_out, ln1_g, ln1_b, w_q, w_k, w_v, w_o, ln2_g, ln2_b, w_router, router_bias, w1_e, w3_e, w2_e, w1_s, w3_s, w2_s, ln3_g, ln3_b):
    assert w_in.shape[0] == 1, "single-layer trunk"
    nb, L, d = x_prompt.shape
    ns = x_sample.shape[0]
    nm = mem_prompt.shape[1]
    row = lambda a: a.reshape(1, -1)

    wp = dict(
        w_in=w_in[0].astype(BF16), b_in=row(b_in[0]),
        w_conv_a=w_conv_a[0], b_conv_a=row(b_conv_a[0]),
        w_gate=jnp.concatenate([_block_diag_dense(w_rg_a[0]), _block_diag_dense(w_rg_x[0])], axis=1).astype(BF16),
        b_gate=row(jnp.concatenate([b_rg_a[0], b_rg_x[0]])),
        lam=row(rg_lambda[0]),
        w_conv_b=w_conv_b[0], b_conv_b=row(b_conv_b[0]),
        ln_c_g=row(ln_c_g[0]), ln_c_b=row(ln_c_b[0]),
        w_out=w_out[0].astype(BF16), ln1_g=row(ln1_g[0]), ln1_b=row(ln1_b[0]),
    )
    wq = w_q[0].astype(BF16)
    wo = w_o[0].astype(BF16)
    wkv = jnp.concatenate([w_k[0], w_v[0]], axis=1).astype(BF16)
    l2g, l2b = row(ln2_g[0]), row(ln2_b[0])

    x1_p, h_p, ca_p, cb_p = _mix_prompt(x_prompt, wp)
    kv = _linear(mem_prompt.reshape(nb * nm, d), wkv, 512, "kv_proj")
    mk_p = kv[:, :d].reshape(nb, nm, d)
    mv_p = kv[:, d:].reshape(nb, nm, d)
    x2_p = _attn_prompt(x1_p, mk_p, mv_p, wq, wo, l2g, l2b)

    xs = x_sample.reshape(ns, d)
    sta = jnp.transpose(state_rglru_conv[0], (1, 0, 2))
    stb = jnp.transpose(state_dwconv[0], (1, 0, 2))
    x1_s, h_s, u_s, glu_s = _mix_sample(xs, sta, stb, state_rglru_h[0], wp)
    q_s = _linear(x1_s, wq, ns, "q_sample")
    o_s = _attn_sample(q_s, cache_mem_k, cache_mem_v)
    x2_s = _linear_res_ln(o_s, x1_s, wo, l2g, l2b, ns, "attn_out_sample")

    x2 = jnp.concatenate([x2_p.reshape(nb * L, d), x2_s], axis=0)
    t = x2.shape[0]
    nseg = MOE_NSEG
    tseg = t // nseg
    assert tseg * nseg == t and tseg % LANES == 0 and t % TOK_TM == 0
    nblk = -(-(tseg * TOP_K + N_EXPERTS * (MOE_R - 1)) // MOE_R)
    nblk += nblk % 2
    nblk_pad = -(-(nblk + 2) // LANES) * LANES
    idx_f, gate_f, rank_f, pstart, blk_e, blk_n = _router(
        x2, jnp.transpose(w_router[0]), router_bias[0].reshape(N_EXPERTS, 1), nseg, nblk_pad)
    plan = _plan_sc(idx_f.reshape(-1), rank_f.reshape(-1), gate_f.reshape(-1), pstart.reshape(-1),
                    nseg, tseg, (nblk + 4) * MOE_R, MOE_R)
    w13_e = jnp.concatenate([w1_e[0], w3_e[0]], axis=-1).astype(BF16)
    w2_eb = w2_e[0].astype(BF16)
    xslab = x2.reshape(nseg, tseg * ROW_SLAB, LANES)
    parts = []
    for si in range(nseg):
        slots = plan[2 * si]
        gates = lax.bitcast_convert_type(plan[2 * si + 1], F32)
        acc = _expert_segment(blk_e[si, 0], blk_n[si, 0], slots, gates, xslab[si], w13_e, w2_eb, tseg, nblk)
        parts.append(acc[:tseg * ROW_SLAB].reshape(tseg, d))
    routed = jnp.concatenate(parts, axis=0)
    w13_s = jnp.concatenate([w1_s[0], w3_s[0]], axis=-1).astype(BF16)
    y = _final(x2, routed, w13_s, w2_s[0].astype(BF16), row(ln3_g[0]), row(ln3_b[0]))

    y_p = y[:nb * L].reshape(nb, L, d)
    y_s = y[nb * L:].reshape(ns, 1, d)
    new_ca_s = jnp.concatenate([state_rglru_conv[0][:, 1:], u_s[:, None, :]], axis=1)
    new_cb_s = jnp.concatenate([state_dwconv[0][:, 1:], glu_s[:, None, :]], axis=1)
    hd = XA_HEAD_DIM
    return (y_p, y_s, h_p[None], ca_p[None], cb_p[None],
            mk_p.reshape(1, nb, nm, XA_HEADS, hd), mv_p.reshape(1, nb, nm, XA_HEADS, hd),
            h_s[None], new_ca_s[None], new_cb_s[None])
```

```python
import functools

import jax
import jax.numpy as jnp
from jax import lax
from jax.experimental import pallas as pl
from jax.experimental.pallas import tpu as pltpu
from jax.experimental.pallas import tpu_sc as plsc

F32 = jnp.float32
BF16 = jnp.bfloat16

D_MODEL = 1024
D_REC = 512
D_CONV = 512
REC_HEADS = 8
REC_HEAD_DIM = D_REC // REC_HEADS
REC_CONV_W = 4
RG_C = 8.0
CONF_CONV_W = 31
XA_HEADS = 4
XA_HEAD_DIM = D_MODEL // XA_HEADS
N_EXPERTS = 64
TOP_K = 8
N_GROUPS = 8
TOPK_GROUPS = 4
GROUP_SIZE = N_EXPERTS // N_GROUPS
D_EXPERT = D_MODEL // 4
ROUTED_SCALE = 2.5
LN_EPS = 1e-5
DEPTH = 1
ALPHA = (2 * DEPTH) ** 0.25

LANES = 128
SUBLANES = 8
VMEM_LIMIT = 56 * 1024 * 1024

MIX_TL = 64
SCAN_PAD = 8
ATT_TM = 512
ATT_G = 4
TOK_TM = 384
MOE_NSEG = 3
MOE_R = 128
ROW_SLAB = D_MODEL // LANES
TILE_S = MOE_R + 8


def _cp(sem=None):
    return pltpu.CompilerParams(dimension_semantics=sem, vmem_limit_bytes=VMEM_LIMIT)


def _ln(x, g, b):
    mu = jnp.mean(x, axis=-1, keepdims=True)
    xc = x - mu
    var = jnp.mean(xc * xc, axis=-1, keepdims=True)
    return xc * lax.rsqrt(var + LN_EPS) * g + b


def _sigmoid(x):
    return 1.0 / (1.0 + jnp.exp(-x))


def _silu(x):
    return x * _sigmoid(x)


def _softplus(x):
    return jnp.maximum(x, 0.0) + jnp.log1p(jnp.exp(-jnp.abs(x)))


def _gelu_tanh(x):
    c = 0.7978845608028654
    return 0.5 * x * (1.0 + jnp.tanh(c * (x + 0.044715 * (x * x * x))))


def _dot(a, b):
    return jnp.dot(a, b, preferred_element_type=F32)


def _full(shape):
    n = len(shape)
    return pl.BlockSpec(shape, lambda *_: (0,) * n)


def _rglru_coeffs(ucf, wg_ref, bg_ref, lam_ref):
    gates = _dot(ucf.astype(BF16), wg_ref[...]) + bg_ref[...]
    r = _sigmoid(gates[:, :D_REC])
    ig = _sigmoid(gates[:, D_REC:])
    log_a = (-RG_C) * r * _softplus(-lam_ref[...])
    a = jnp.exp(log_a)
    bt = jnp.sqrt(-jnp.tanh(log_a) * (a * a + 1.0)) * (ig * ucf)
    return a, bt


def _mix_prompt_kernel(x_ref, win_ref, bin_ref, wca_ref, bca_ref, wg_ref, bg_ref, lam_ref,
                       wcb_ref, bcb_ref, lcg_ref, lcb_ref, wout_ref, l1g_ref, l1b_ref,
                       x1_ref, hl_ref, ca_ref, cb_ref,
                       xs, exta, extb, hc, *, nb, tl):
    i = pl.program_id(0)
    m = nb * tl
    wa = REC_CONV_W
    wb = CONF_CONV_W
    ha = (wa - 1) * nb
    hb = (wb - 1) * nb
    nch = D_MODEL // LANES

    @pl.when(i == 0)
    def _():
        exta[0:ha, :] = jnp.zeros((ha, D_REC), F32)
        extb[0:hb, :] = jnp.zeros((hb, D_CONV), F32)
        hc[...] = jnp.zeros((nb, D_REC), F32)

    for n in range(nb):
        for c in range(nch):
            xs[c, pl.ds(n, tl, stride=nb), :] = x_ref[n, :, c * LANES:(c + 1) * LANES]
    xf = jnp.concatenate([xs[c] for c in range(nch)], axis=-1)
    z = _dot(xf.astype(BF16), win_ref[...]) + bin_ref[...]
    g = z[:, 0:D_REC]
    u = z[:, D_REC:2 * D_REC]
    v = z[:, 2 * D_REC:2 * D_REC + D_CONV]
    gg = z[:, 2 * D_REC + D_CONV:]

    exta[ha:ha + m, :] = u
    uc = jnp.zeros((m, D_REC), F32) + bca_ref[...]
    for j in range(wa):
        uc = uc + wca_ref[j:j + 1, :] * exta[j * nb:j * nb + m, :]
    tail_a = exta[m:m + ha, :]
    ca_ref[...] = tail_a.reshape(wa - 1, nb, D_REC)
    exta[0:ha, :] = tail_a

    a, bt = _rglru_coeffs(uc, wg_ref, bg_ref, lam_ref)

    h = hc[...]
    hs = []
    for t in range(tl):
        h = a[t * nb:(t + 1) * nb, :] * h + bt[t * nb:(t + 1) * nb, :]
        hs.append(h)
    hc[...] = h
    hl_ref[...] = h
    y_a = _gelu_tanh(g) * jnp.concatenate(hs, axis=0)

    extb[hb:hb + m, :] = v * _sigmoid(gg)
    c = jnp.zeros((m, D_CONV), F32) + bcb_ref[...]
    for j in range(wb):
        c = c + wcb_ref[j:j + 1, :] * extb[j * nb:j * nb + m, :]
    tail_b = extb[m:m + hb, :]
    cb_ref[...] = tail_b.reshape(wb - 1, nb, D_CONV)
    extb[0:hb, :] = tail_b
    y_b = _silu(_ln(c, lcg_ref[...], lcb_ref[...]))

    mix = jnp.concatenate([y_a, y_b], axis=-1).astype(BF16)
    o = _dot(mix, wout_ref[...])
    x1 = _ln(ALPHA * xf + o, l1g_ref[...], l1b_ref[...])
    for c in range(nch):
        xs[c] = x1[:, c * LANES:(c + 1) * LANES]
    for n in range(nb):
        for c in range(nch):
            x1_ref[n, :, c * LANES:(c + 1) * LANES] = xs[c, pl.ds(n, tl, stride=nb), :]


def _mix_prompt(x, wp):
    nb, L, _ = x.shape
    tl = MIX_TL
    m = nb * tl
    kern = functools.partial(_mix_prompt_kernel, nb=nb, tl=tl)
    ins = [wp['w_in'], wp['b_in'], wp['w_conv_a'], wp['b_conv_a'], wp['w_gate'], wp['b_gate'],
           wp['lam'], wp['w_conv_b'], wp['b_conv_b'], wp['ln_c_g'], wp['ln_c_b'], wp['w_out'],
           wp['ln1_g'], wp['ln1_b']]
    x1, h, ca, cb = pl.pallas_call(
        kern,
        grid=(L // tl,),
        in_specs=[pl.BlockSpec((nb, tl, D_MODEL), lambda i: (0, i, 0))] + [_full(a.shape) for a in ins],
        out_specs=[pl.BlockSpec((nb, tl, D_MODEL), lambda i: (0, i, 0)),
                   _full((nb, D_REC)), _full((REC_CONV_W - 1, nb, D_REC)),
                   _full((CONF_CONV_W - 1, nb, D_CONV))],
        out_shape=[jax.ShapeDtypeStruct((nb, L, D_MODEL), F32),
                   jax.ShapeDtypeStruct((nb, D_REC), F32),
                   jax.ShapeDtypeStruct((REC_CONV_W - 1, nb, D_REC), F32),
                   jax.ShapeDtypeStruct((CONF_CONV_W - 1, nb, D_CONV), F32)],
        scratch_shapes=[pltpu.VMEM((D_MODEL // LANES, m, LANES), F32),
                        pltpu.VMEM(((REC_CONV_W - 1) * nb + m, D_REC), F32),
                        pltpu.VMEM(((CONF_CONV_W - 1) * nb + m, D_CONV), F32),
                        pltpu.VMEM((nb, D_REC), F32)],
        compiler_params=_cp(("arbitrary",)),
        name="mix_prompt",
    )(x, *ins)
    return x1, h, jnp.transpose(ca, (1, 0, 2)), jnp.transpose(cb, (1, 0, 2))


def _mix_sample_kernel(x_ref, sta_ref, stb_ref, h0_ref, win_ref, bin_ref, wca_ref, bca_ref, wg_ref,
                       bg_ref, lam_ref, wcb_ref, bcb_ref, lcg_ref, lcb_ref, wout_ref, l1g_ref, l1b_ref,
                       x1_ref, h_ref, u_ref, glu_ref):
    wa = REC_CONV_W
    wb = CONF_CONV_W
    xf = x_ref[...]
    z = _dot(xf.astype(BF16), win_ref[...]) + bin_ref[...]
    g = z[:, 0:D_REC]
    u = z[:, D_REC:2 * D_REC]
    v = z[:, 2 * D_REC:2 * D_REC + D_CONV]
    gg = z[:, 2 * D_REC + D_CONV:]

    uc = bca_ref[...] + wca_ref[wa - 1:wa, :] * u
    for j in range(wa - 1):
        uc = uc + wca_ref[j:j + 1, :] * sta_ref[j]
    a, bt = _rglru_coeffs(uc, wg_ref, bg_ref, lam_ref)
    h = a * h0_ref[...] + bt
    y_a = _gelu_tanh(g) * h

    glu = v * _sigmoid(gg)
    c = bcb_ref[...] + wcb_ref[wb - 1:wb, :] * glu
    for j in range(wb - 1):
        c = c + wcb_ref[j:j + 1, :] * stb_ref[j]
    y_b = _silu(_ln(c, lcg_ref[...], lcb_ref[...]))

    mix = jnp.concatenate([y_a, y_b], axis=-1).astype(BF16)
    o = _dot(mix, wout_ref[...])
    x1_ref[...] = _ln(ALPHA * xf + o, l1g_ref[...], l1b_ref[...])
    h_ref[...] = h
    u_ref[...] = u
    glu_ref[...] = glu


def _mix_sample(x, sta, stb, h0, wp):
    n = x.shape[0]
    ins = [x, sta, stb, h0, wp['w_in'], wp['b_in'], wp['w_conv_a'], wp['b_conv_a'], wp['w_gate'],
           wp['b_gate'], wp['lam'], wp['w_conv_b'], wp['b_conv_b'], wp['ln_c_g'], wp['ln_c_b'],
           wp['w_out'], wp['ln1_g'], wp['ln1_b']]
    return pl.pallas_call(
        _mix_sample_kernel,
        grid=(1,),
        in_specs=[_full(a.shape) for a in ins],
        out_specs=[_full((n, D_MODEL)), _full((n, D_REC)), _full((n, D_REC)), _full((n, D_CONV))],
        out_shape=[jax.ShapeDtypeStruct((n, D_MODEL), F32), jax.ShapeDtypeStruct((n, D_REC), F32),
                   jax.ShapeDtypeStruct((n, D_REC), F32), jax.ShapeDtypeStruct((n, D_CONV), F32)],
        compiler_params=_cp(("arbitrary",)),
        name="mix_sample",
    )(*ins)


def _linear_kernel(a_ref, w_ref, o_ref):
    o_ref[...] = _dot(a_ref[...].astype(BF16), w_ref[...])


def _linear(a, w, tm, name):
    m, k = a.shape
    n = w.shape[1]
    return pl.pallas_call(
        _linear_kernel,
        grid=(m // tm,),
        in_specs=[pl.BlockSpec((tm, k), lambda i: (i, 0)), _full(w.shape)],
        out_specs=pl.BlockSpec((tm, n), lambda i: (i, 0)),
        out_shape=jax.ShapeDtypeStruct((m, n), F32),
        compiler_params=_cp(("parallel",)),
        name=name,
    )(a, w)


def _attn_out_sample_kernel(a_ref, r_ref, w_ref, g_ref, b_ref, rows_in, slab_in, o_ref, slab_ref):
    del rows_in, slab_in
    o = _dot(a_ref[...].astype(BF16), w_ref[...])
    _store_rows_and_slab(_ln(ALPHA * r_ref[...] + o, g_ref[...], b_ref[...]), o_ref, slab_ref)


def _attn_out_sample(a, res, w, g, b, x2_rows, x2_slab):
    m, k = a.shape
    n = w.shape[1]
    blk = x2_rows.shape[0] // m - 1
    assert (blk + 1) * m == x2_rows.shape[0]
    any_spec = pl.BlockSpec(memory_space=pl.ANY)
    return pl.pallas_call(
        _attn_out_sample_kernel,
        grid=(1,),
        in_specs=[_full(a.shape), _full(res.shape), _full(w.shape), _full(g.shape), _full(b.shape),
                  any_spec, any_spec],
        out_specs=[pl.BlockSpec((m, n), lambda i: (blk, 0)),
                   pl.BlockSpec((m * ROW_SLAB, LANES), lambda i: (blk, 0))],
        out_shape=[jax.ShapeDtypeStruct(x2_rows.shape, F32), jax.ShapeDtypeStruct(x2_slab.shape, F32)],
        input_output_aliases={5: 0, 6: 1},
        compiler_params=_cp(("arbitrary",)),
        name="attn_out_sample",
    )(a, res, w, g, b, x2_rows, x2_slab)


def _attn_prompt_kernel(x_ref, k_ref, v_ref, wq_ref, wo_ref, g_ref, b_ref, o_ref, slab_ref):
    x = x_ref[0]
    q = _dot(x.astype(BF16), wq_ref[...])
    k = k_ref[0].astype(BF16)
    v = v_ref[0].astype(BF16)
    scale = XA_HEAD_DIM ** -0.5
    outs = []
    for h in range(XA_HEADS):
        sl = slice(h * XA_HEAD_DIM, (h + 1) * XA_HEAD_DIM)
        s = lax.dot_general(q[:, sl].astype(BF16), k[:, sl], (((1,), (1,)), ((), ())),
                            preferred_element_type=F32) * scale
        e = jnp.exp(s - jnp.max(s, axis=-1, keepdims=True))
        p = e / jnp.sum(e, axis=-1, keepdims=True)
        outs.append(_dot(p.astype(BF16), v[:, sl]))
    o = jnp.concatenate(outs, axis=-1).astype(BF16)
    att = _dot(o, wo_ref[...])
    _store_rows_and_slab(_ln(ALPHA * x + att, g_ref[...], b_ref[...]), o_ref, slab_ref)


def _store_rows_and_slab(x2, o_ref, slab_ref):
    rows = x2.shape[0]
    o_ref[...] = x2
    for c in range(ROW_SLAB):
        slab_ref[pl.ds(c, rows, stride=ROW_SLAB), :] = x2[:, c * LANES:(c + 1) * LANES]


def _attn_prompt(x1, mk, mv, wq, wo, g, b, t_all):
    nb, L, d = x1.shape
    nm = mk.shape[1]
    tm = ATT_TM
    per = L // tm
    return pl.pallas_call(
        _attn_prompt_kernel,
        grid=(nb, per),
        in_specs=[pl.BlockSpec((1, tm, d), lambda bi, i: (bi, i, 0)),
                  pl.BlockSpec((1, nm, d), lambda bi, i: (bi, 0, 0)),
                  pl.BlockSpec((1, nm, d), lambda bi, i: (bi, 0, 0)),
                  _full(wq.shape), _full(wo.shape), _full(g.shape), _full(b.shape)],
        out_specs=[pl.BlockSpec((tm, d), lambda bi, i: (bi * per + i, 0)),
                   pl.BlockSpec((tm * ROW_SLAB, LANES), lambda bi, i: (bi * per + i, 0))],
        out_shape=[jax.ShapeDtypeStruct((t_all, d), F32),
                   jax.ShapeDtypeStruct((t_all * ROW_SLAB, LANES), F32)],
        compiler_params=_cp(("parallel", "parallel")),
        name="attn_prompt",
    )(x1, mk, mv, wq, wo, g, b)


def _attn_sample_kernel(q_ref, k_ref, v_ref, o_ref):
    scale = XA_HEAD_DIM ** -0.5
    for gi in range(ATT_G):
        qn = q_ref[gi]
        s = jnp.sum(k_ref[gi] * qn[None], axis=-1, keepdims=True) * scale
        e = jnp.exp(s - jnp.max(s, axis=0, keepdims=True))
        p = e / jnp.sum(e, axis=0, keepdims=True)
        o_ref[gi] = jnp.sum(p * v_ref[gi], axis=0)


def _attn_sample(q, ck, cv):
    _, n, nm, nh, dh = ck.shape
    q3 = q.reshape(n, nh, dh)
    cache_spec = pl.BlockSpec((None, ATT_G, nm, nh, dh), lambda i: (0, i, 0, 0, 0))
    out = pl.pallas_call(
        _attn_sample_kernel,
        grid=(n // ATT_G,),
        in_specs=[pl.BlockSpec((ATT_G, nh, dh), lambda i: (i, 0, 0)), cache_spec, cache_spec],
        out_specs=pl.BlockSpec((ATT_G, nh, dh), lambda i: (i, 0, 0)),
        out_shape=jax.ShapeDtypeStruct((n, nh, dh), F32),
        compiler_params=_cp(("parallel",)),
        name="attn_sample",
    )(q3, ck, cv)
    return out.reshape(n, nh * dh)


def _first_argmax(val, iota, n, axis):
    mx = jnp.max(val, axis=axis, keepdims=True)
    idx = jnp.min(jnp.where(val == mx, iota, n), axis=axis, keepdims=True)
    return mx, idx


def _router_kernel(x_ref, wr_ref, rb_ref, idx_ref, gate_ref, rank_ref, ps_ref, be_ref, bn_ref, carry):
    tm = x_ref.shape[0]
    neg = -jnp.inf

    @pl.when(pl.program_id(1) == 0)
    def _():
        carry[...] = jnp.zeros(carry.shape, F32)

    logits = lax.dot_general(wr_ref[...], x_ref[...], (((1,), (1,)), ((), ())),
                             precision=lax.Precision.HIGHEST, preferred_element_type=F32)
    scores = _sigmoid(logits)
    biased = scores + rb_ref[...]
    grp = biased.reshape(N_GROUPS, GROUP_SIZE, tm)
    io_g = lax.broadcasted_iota(jnp.int32, (N_GROUPS, GROUP_SIZE, tm), 1)
    m1, i1 = _first_argmax(grp, io_g, GROUP_SIZE, 1)
    m2 = jnp.max(jnp.where(io_g == i1, neg, grp), axis=1, keepdims=True)
    gscore = (m1 + m2).reshape(N_GROUPS, tm)
    io_n = lax.broadcasted_iota(jnp.int32, (N_GROUPS, tm), 0)
    gsel = jnp.zeros((N_GROUPS, tm), jnp.bool_)
    for _ in range(TOPK_GROUPS):
        _, gi = _first_argmax(gscore, io_n, N_GROUPS, 0)
        hit = io_n == gi
        gsel = jnp.logical_or(gsel, hit)
        gscore = jnp.where(hit, neg, gscore)
    emask = jnp.broadcast_to(gsel.reshape(N_GROUPS, 1, tm), (N_GROUPS, GROUP_SIZE, tm)).reshape(N_EXPERTS, tm)
    val = jnp.where(emask, biased, neg)
    io_e = lax.broadcasted_iota(jnp.int32, (N_EXPERTS, tm), 0)
    idxs, ws, hits = [], [], []
    for _ in range(TOP_K):
        _, ei = _first_argmax(val, io_e, N_EXPERTS, 0)
        hit = io_e == ei
        ws.append(jnp.sum(jnp.where(hit, scores, 0.0), axis=0, keepdims=True))
        idxs.append(ei)
        hits.append(hit)
        val = jnp.where(hit, neg, val)
    w = jnp.concatenate(ws, axis=0)
    idx_ref[...] = jnp.concatenate(idxs, axis=0)
    gate_ref[...] = w / jnp.sum(w, axis=0, keepdims=True) * ROUTED_SCALE

    onehot = jnp.zeros((N_EXPERTS, tm), F32)
    for hit in hits:
        onehot = onehot + hit.astype(F32)
    tri = (lax.broadcasted_iota(jnp.int32, (tm, tm), 0) <= lax.broadcasted_iota(jnp.int32, (tm, tm), 1))
    csum = _dot(onehot.astype(BF16), tri.astype(BF16)) + carry[...]
    ranks = [jnp.sum(jnp.where(hit, csum, 0.0), axis=0, keepdims=True) - 1.0 for hit in hits]
    rank_ref[...] = jnp.concatenate(ranks, axis=0).astype(jnp.int32)
    counts = csum[:, tm - 1:tm]
    carry[...] = counts

    @pl.when(pl.program_id(1) == pl.num_programs(1) - 1)
    def _():
        r = float(MOE_R)
        padded = jnp.floor((counts + (r - 1.0)) / r) * r
        below = (lax.broadcasted_iota(jnp.int32, (N_EXPERTS, N_EXPERTS), 1)
                 < lax.broadcasted_iota(jnp.int32, (N_EXPERTS, N_EXPERTS), 0)).astype(F32)
        pstart = jnp.dot(below, jnp.broadcast_to(padded, (N_EXPERTS, LANES)), precision=lax.Precision.HIGHEST,
                         preferred_element_type=F32)[:, 0:1]
        pend = pstart + padded
        nb = be_ref.shape[-1]
        bpos = lax.broadcasted_iota(jnp.int32, (1, nb), 1).astype(F32) * r
        be = jnp.minimum(jnp.sum((bpos >= pend).astype(F32), axis=0, keepdims=True), float(N_EXPERTS - 1))
        sel = lax.broadcasted_iota(jnp.int32, (N_EXPERTS, nb), 0) == be.astype(jnp.int32)
        cnt_b = jnp.sum(jnp.where(sel, counts, 0.0), axis=0, keepdims=True)
        st_b = jnp.sum(jnp.where(sel, pstart, 0.0), axis=0, keepdims=True)
        ps_ref[0] = pstart.astype(jnp.int32)
        be_ref[0] = be.astype(jnp.int32)
        bn_ref[0] = jnp.clip(cnt_b - (bpos - st_b), 0.0, r).astype(jnp.int32)


def _router(x2, wr_t, rb, nseg, nblk_pad):
    t, d = x2.shape
    tm = LANES
    tiles = t // (nseg * tm)
    tok_spec = pl.BlockSpec((TOP_K, tm), lambda s, j: (s * tiles + j, 0))
    tab_spec = pl.BlockSpec((1, 1, nblk_pad), lambda s, j: (s, 0, 0))
    tok_shape = (t // tm * TOP_K, tm)
    return pl.pallas_call(
        _router_kernel,
        grid=(nseg, tiles),
        in_specs=[pl.BlockSpec((tm, d), lambda s, j: (s * tiles + j, 0)), _full(wr_t.shape), _full(rb.shape)],
        out_specs=[tok_spec, tok_spec, tok_spec, pl.BlockSpec((1, N_EXPERTS, 1), lambda s, j: (s, 0, 0)),
                   tab_spec, tab_spec],
        out_shape=[jax.ShapeDtypeStruct(tok_shape, jnp.int32), jax.ShapeDtypeStruct(tok_shape, F32),
                   jax.ShapeDtypeStruct(tok_shape, jnp.int32),
                   jax.ShapeDtypeStruct((nseg, N_EXPERTS, 1), jnp.int32),
                   jax.ShapeDtypeStruct((nseg, 1, nblk_pad), jnp.int32),
                   jax.ShapeDtypeStruct((nseg, 1, nblk_pad), jnp.int32)],
        scratch_shapes=[pltpu.VMEM((N_EXPERTS, 1), F32)],
        compiler_params=_cp(("arbitrary", "arbitrary")),
        name="router",
    )(x2, wr_t, rb)


def _plan_sc(idx_f, rank_f, gate_f, pstart_f, nseg, tseg, nslots, lead):
    words = tseg * TOP_K
    chunk = words // 4
    assert chunk * 4 == words and chunk % 16 == 0 and nslots % 16 == 0
    info = plsc.get_sparse_core_info()
    ncores = info.num_cores
    nl = info.num_lanes
    mesh = plsc.VectorSubcoreMesh(core_axis_name="c", subcore_axis_name="s")
    dummy_off = tseg * ROW_SLAB

    def body(idx_hbm, rank_hbm, gate_hbm, ps_hbm, *rest):
        outs = rest[:2 * nseg]
        idx_v, rank_v, val_v, ps_v, slot_v = rest[2 * nseg:]
        wid = lax.axis_index("s") * ncores + lax.axis_index("c")
        lane = lax.iota(jnp.int32, nl)

        def run(seg, want_gate):
            out_hbm = outs[2 * seg + (1 if want_gate else 0)]
            pltpu.sync_copy(ps_hbm.at[pl.ds(seg * N_EXPERTS, N_EXPERTS)], ps_v)
            fill = jnp.full((nl,), 0 if want_gate else dummy_off, jnp.int32)

            @pl.loop(0, nslots, step=nl)
            def _(i):
                slot_v[pl.ds(i, nl)] = fill

            @pl.loop(0, 4)
            def _(ci):
                base = seg * words + ci * chunk
                pltpu.sync_copy(idx_hbm.at[pl.ds(base, chunk)], idx_v)
                pltpu.sync_copy(rank_hbm.at[pl.ds(base, chunk)], rank_v)
                if want_gate:
                    pltpu.sync_copy(gate_hbm.at[pl.ds(base, chunk)], val_v)

                @pl.loop(0, chunk, step=nl)
                def _(o):
                    e = idx_v[pl.ds(o, nl)]
                    d = plsc.load_gather(ps_v, [e]) + rank_v[pl.ds(o, nl)] + lead
                    if want_gate:
                        plsc.store_scatter(slot_v, [d], val_v[pl.ds(o, nl)])
                    else:
                        f = ci * chunk + o + lane
                        tok = ((f >> 10) << 7) | (f & (LANES - 1))
                        plsc.store_scatter(slot_v, [d], tok * ROW_SLAB)

            pltpu.sync_copy(slot_v, out_hbm)

        for seg in range(nseg):
            for want_gate in (False, True):
                @pl.when(wid == 2 * seg + (1 if want_gate else 0))
                def _(seg=seg, want_gate=want_gate):
                    run(seg, want_gate)

    out_type = [jax.ShapeDtypeStruct((nslots,), jnp.int32) for _ in range(2 * nseg)]
    fn = pl.kernel(
        body, out_type=out_type, mesh=mesh,
        scratch_types=[pltpu.VMEM((chunk,), jnp.int32), pltpu.VMEM((chunk,), jnp.int32),
                       pltpu.VMEM((chunk,), jnp.int32), pltpu.VMEM((N_EXPERTS,), jnp.int32),
                       pltpu.VMEM((nslots,), jnp.int32)],
        compiler_params=pltpu.CompilerParams(needs_layout_passes=False), name="plan_sc")
    return fn(idx_f, rank_f, lax.bitcast_convert_type(gate_f, jnp.int32), pstart_f)


def _expert_kernel(be_ref, bn_ref, slot_ref, gate_ref, x_ref, w13e_ref, w2e_ref, w13o_ref, w2o_ref,
                   *rest, tseg):
    acc_ref, tile_a, tile_b, ys_a, ys_b = rest[-5:]
    _expert_body(bn_ref, slot_ref, gate_ref, x_ref, w13e_ref, w2e_ref, w13o_ref, w2o_ref,
                 acc_ref, tile_a, tile_b, ys_a, ys_b, tseg)
    del be_ref


def _expert_body(bn_ref, slot_ref, gate_ref, x_ref, w13e_ref, w2e_ref, w13o_ref, w2o_ref,
                 acc_ref, tile_a, tile_b, ys_a, ys_b, tseg):
    i = pl.program_id(0)
    r = MOE_R
    s = TILE_S
    zero_rows = 64 * ROW_SLAB
    last_off = (tseg - 1) * ROW_SLAB

    def gather(blk, tile):
        base = (blk + 1) * r
        for row in range(r):
            off = pl.multiple_of(jnp.minimum(slot_ref[base + row], last_off), ROW_SLAB)
            tile[pl.ds(row, ROW_SLAB, stride=s), :] = x_ref[pl.ds(off, ROW_SLAB), :]

    def ffn(tile, w13_ref, w2_ref, ys):
        xb = jnp.concatenate([tile[c * s:c * s + r, :] for c in range(ROW_SLAB)], axis=-1).astype(BF16)
        h13 = _dot(xb, w13_ref[0])
        hh = (_silu(h13[:, :D_EXPERT]) * h13[:, D_EXPERT:]).astype(BF16)
        y = _dot(hh, w2_ref[0])
        for g in range(r // SUBLANES):
            for c in range(ROW_SLAB):
                ys[pl.ds((g * ROW_SLAB + c) * SUBLANES, SUBLANES), :] = (
                    y[g * SUBLANES:(g + 1) * SUBLANES, c * LANES:(c + 1) * LANES])

    def scatter(blk, ys):
        base = (blk + 1) * r
        for q in range(r // 4):
            offs, vals = [], []
            for row in range(q * 4, q * 4 + 4):
                off = pl.multiple_of(slot_ref[base + row], ROW_SLAB)
                src = (row // SUBLANES) * (ROW_SLAB * SUBLANES) + row % SUBLANES
                vals.append(acc_ref[pl.ds(off, ROW_SLAB), :]
                            + ys[pl.ds(src, ROW_SLAB, stride=SUBLANES), :] * gate_ref[base + row])
                offs.append(off)
            for off, val in zip(offs, vals):
                acc_ref[pl.ds(off, ROW_SLAB), :] = val

    @pl.when(i == 0)
    def _():
        def zero(k, c):
            acc_ref[pl.ds(pl.multiple_of(k * zero_rows, zero_rows), zero_rows), :] = jnp.zeros((zero_rows, LANES), F32)
            return c

        lax.fori_loop(0, acc_ref.shape[0] // zero_rows, zero, 0)
        rem = acc_ref.shape[0] % zero_rows
        if rem:
            acc_ref[pl.ds(acc_ref.shape[0] - rem, rem), :] = jnp.zeros((rem, LANES), F32)
        tile_a[...] = jnp.zeros(tile_a.shape, F32)
        tile_b[...] = jnp.zeros(tile_b.shape, F32)
        ys_b[...] = jnp.zeros(ys_b.shape, F32)
        gather(0, tile_a)

    even = 2 * i
    odd = even + 1
    pending = bn_ref[even] + bn_ref[odd] + bn_ref[jnp.maximum(even - 1, 0)]

    @pl.when(pending > 0)
    def _():
        gather(odd, tile_b)
        ffn(tile_a, w13e_ref, w2e_ref, ys_a)
        scatter(even - 1, ys_b)
        gather(even + 2, tile_a)
        ffn(tile_b, w13o_ref, w2o_ref, ys_b)
        scatter(even, ys_a)


def _expert_segment(si, nseg, blk_e, blk_n, slots, gates, xslab3, w13, w2, tseg, nblk, routed_prev):
    r = MOE_R
    acc_rows = (tseg + SUBLANES) * ROW_SLAB
    one = pl.Buffered(1)
    w13_blk = (1, D_MODEL, 2 * D_EXPERT)
    w2_blk = (1, D_EXPERT, D_MODEL)
    in_specs = [pl.BlockSpec((None, tseg * ROW_SLAB, LANES), lambda i, *_: (si, 0, 0), pipeline_mode=one),
                pl.BlockSpec(w13_blk, lambda i, be, *_: (be[2 * i], 0, 0)),
                pl.BlockSpec(w2_blk, lambda i, be, *_: (be[2 * i], 0, 0)),
                pl.BlockSpec(w13_blk, lambda i, be, *_: (be[2 * i + 1], 0, 0)),
                pl.BlockSpec(w2_blk, lambda i, be, *_: (be[2 * i + 1], 0, 0))]
    args = [blk_e, blk_n, slots, gates, xslab3, w13, w2, w13, w2]
    aliases = {}
    if routed_prev is not None:
        in_specs.append(pl.BlockSpec(memory_space=pl.ANY))
        aliases = {len(args): 0}
        args.append(routed_prev)
    grid_spec = pltpu.PrefetchScalarGridSpec(
        num_scalar_prefetch=4,
        grid=(nblk // 2 + 1,),
        in_specs=in_specs,
        out_specs=pl.BlockSpec((None, acc_rows, LANES), lambda i, *_: (si, 0, 0), pipeline_mode=one),
        scratch_shapes=[pltpu.VMEM((ROW_SLAB * TILE_S, LANES), F32),
                        pltpu.VMEM((ROW_SLAB * TILE_S, LANES), F32),
                        pltpu.VMEM((r * ROW_SLAB, LANES), F32),
                        pltpu.VMEM((r * ROW_SLAB, LANES), F32)],
    )
    return pl.pallas_call(
        functools.partial(_expert_kernel, tseg=tseg),
        grid_spec=grid_spec,
        out_shape=jax.ShapeDtypeStruct((nseg, acc_rows, LANES), F32),
        input_output_aliases=aliases,
        compiler_params=_cp(("arbitrary",)),
        name="experts",
    )(*args)


def _final_kernel(x_ref, r_ref, w13_ref, w2_ref, g_ref, b_ref, yp_ref, ys_ref, *, tiles, p_tiles):
    tm = x_ref.shape[0]
    x = x_ref[...]
    h13 = _dot(x.astype(BF16), w13_ref[...])
    hh = (_silu(h13[:, :D_EXPERT]) * h13[:, D_EXPERT:]).astype(BF16)
    sh = _dot(hh, w2_ref[...])
    routed = jnp.concatenate([r_ref[pl.ds(c, tm, stride=ROW_SLAB), :] for c in range(ROW_SLAB)], axis=-1)
    y = _ln(ALPHA * x + (sh + routed), g_ref[...], b_ref[...])
    tile = pl.program_id(0) * tiles + pl.program_id(1)

    @pl.when(tile < p_tiles)
    def _():
        yp_ref[...] = y

    @pl.when(tile >= p_tiles)
    def _():
        ys_ref[...] = y


def _final(x2, routed3, w13, w2, g, b, n_prompt):
    t, d = x2.shape
    nseg = routed3.shape[0]
    tm = LANES
    tiles = t // (nseg * tm)
    p_tiles = n_prompt // tm
    n_sample = t - n_prompt
    assert p_tiles * tm == n_prompt and n_sample == tm
    return pl.pallas_call(
        functools.partial(_final_kernel, tiles=tiles, p_tiles=p_tiles),
        grid=(nseg, tiles),
        in_specs=[pl.BlockSpec((tm, d), lambda s, j: (s * tiles + j, 0)),
                  pl.BlockSpec((None, tm * ROW_SLAB, LANES), lambda s, j: (s, j, 0)),
                  _full(w13.shape), _full(w2.shape), _full(g.shape), _full(b.shape)],
        out_specs=[pl.BlockSpec((tm, d), lambda s, j: (jnp.minimum(s * tiles + j, p_tiles - 1), 0)),
                   pl.BlockSpec((tm, d), lambda s, j: (0, 0))],
        out_shape=[jax.ShapeDtypeStruct((n_prompt, d), F32), jax.ShapeDtypeStruct((n_sample, d), F32)],
        compiler_params=_cp(("arbitrary", "arbitrary")),
        name="final",
    )(x2, routed3, w13, w2, g, b)


def _block_diag_dense(w):
    h, dh, _ = w.shape
    eye = jnp.eye(h, dtype=w.dtype)
    return jnp.einsum('hij,hg->higj', w, eye).reshape(h * dh, h * dh)


def kernel(x_prompt, x_sample, mem_prompt, state_rglru_h, state_rglru_conv, state_dwconv, cache_mem_k, cache_mem_v, w_in, b_in, w_conv_a, b_conv_a, w_rg_a, b_rg_a, w_rg_x, b_rg_x, rg_lambda, w_conv_b, b_conv_b, ln_c_g, ln_c_b, w_out, ln1_g, ln1_b, w_q, w_k, w_v, w_o, ln2_g, ln2_b, w_router, router_bias, w1_e, w3_e, w2_e, w1_s, w3_s, w2_s, ln3_g, ln3_b):
    assert w_in.shape[0] == 1, "single-layer trunk"
    nb, L, d = x_prompt.shape
    ns = x_sample.shape[0]
    nm = mem_prompt.shape[1]
    row = lambda a: a.reshape(1, -1)

    wp = dict(
        w_in=w_in[0].astype(BF16), b_in=row(b_in[0]),
        w_conv_a=w_conv_a[0], b_conv_a=row(b_conv_a[0]),
        w_gate=jnp.concatenate([_block_diag_dense(w_rg_a[0]), _block_diag_dense(w_rg_x[0])], axis=1).astype(BF16),
        b_gate=row(jnp.concatenate([b_rg_a[0], b_rg_x[0]])),
        lam=row(rg_lambda[0]),
        w_conv_b=w_conv_b[0], b_conv_b=row(b_conv_b[0]),
        ln_c_g=row(ln_c_g[0]), ln_c_b=row(ln_c_b[0]),
        w_out=w_out[0].astype(BF16), ln1_g=row(ln1_g[0]), ln1_b=row(ln1_b[0]),
    )
    wq = w_q[0].astype(BF16)
    wo = w_o[0].astype(BF16)
    wkv = jnp.concatenate([w_k[0], w_v[0]], axis=1).astype(BF16)
    l2g, l2b = row(ln2_g[0]), row(ln2_b[0])

    x1_p, h_p, ca_p, cb_p = _mix_prompt(x_prompt, wp)
    kv = _linear(mem_prompt.reshape(nb * nm, d), wkv, 512, "kv_proj")
    mk_p = kv[:, :d].reshape(nb, nm, d)
    mv_p = kv[:, d:].reshape(nb, nm, d)
    t = nb * L + ns
    x2, x2_slab = _attn_prompt(x1_p, mk_p, mv_p, wq, wo, l2g, l2b, t)

    xs = x_sample.reshape(ns, d)
    sta = jnp.transpose(state_rglru_conv[0], (1, 0, 2))
    stb = jnp.transpose(state_dwconv[0], (1, 0, 2))
    x1_s, h_s, u_s, glu_s = _mix_sample(xs, sta, stb, state_rglru_h[0], wp)
    q_s = _linear(x1_s, wq, ns, "q_sample")
    o_s = _attn_sample(q_s, cache_mem_k, cache_mem_v)
    x2, x2_slab = _attn_out_sample(o_s, x1_s, wo, l2g, l2b, x2, x2_slab)

    nseg = MOE_NSEG
    tseg = t // nseg
    assert tseg * nseg == t and tseg % LANES == 0 and t % TOK_TM == 0
    nblk = -(-(tseg * TOP_K + N_EXPERTS * (MOE_R - 1)) // MOE_R)
    nblk += nblk % 2
    nblk_pad = -(-(nblk + 2) // LANES) * LANES
    idx_f, gate_f, rank_f, pstart, blk_e, blk_n = _router(
        x2, jnp.transpose(w_router[0]), router_bias[0].reshape(N_EXPERTS, 1), nseg, nblk_pad)
    plan = _plan_sc(idx_f.reshape(-1), rank_f.reshape(-1), gate_f.reshape(-1), pstart.reshape(-1),
                    nseg, tseg, (nblk + 4) * MOE_R, MOE_R)
    w13_e = jnp.concatenate([w1_e[0], w3_e[0]], axis=-1).astype(BF16)
    w2_eb = w2_e[0].astype(BF16)
    xslab3 = x2_slab.reshape(nseg, tseg * ROW_SLAB, LANES)
    routed3 = None
    for si in range(nseg):
        slots = plan[2 * si]
        gates = lax.bitcast_convert_type(plan[2 * si + 1], F32)
        routed3 = _expert_segment(si, nseg, blk_e[si, 0], blk_n[si, 0], slots, gates, xslab3, w13_e, w2_eb,
                                  tseg, nblk, routed3)
    w13_s = jnp.concatenate([w1_s[0], w3_s[0]], axis=-1).astype(BF16)
    y_p, y_s = _final(x2, routed3, w13_s, w2_s[0].astype(BF16), row(ln3_g[0]), row(ln3_b[0]), nb * L)
    y_p = y_p.reshape(nb, L, d)
    y_s = y_s.reshape(ns, 1, d)
    new_ca_s = jnp.concatenate([state_rglru_conv[0][:, 1:], u_s[:, None, :]], axis=1)
    new_cb_s = jnp.concatenate([state_dwconv[0][:, 1:], glu_s[:, None, :]], axis=1)
    hd = XA_HEAD_DIM
    return (y_p, y_s, h_p[None], ca_p[None], cb_p[None],
            mk_p.reshape(1, nb, nm, XA_HEADS, hd), mv_p.reshape(1, nb, nm, XA_HEADS, hd),
            h_s[None], new_ca_s[None], new_cb_s[None])
```

```python
import functools

import jax
import jax.numpy as jnp
from jax import lax
from jax.experimental import pallas as pl
from jax.experimental.pallas import tpu as pltpu
from jax.experimental.pallas import tpu_sc as plsc

F32 = jnp.float32
BF16 = jnp.bfloat16

D_MODEL = 1024
D_REC = 512
D_CONV = 512
REC_HEADS = 8
REC_HEAD_DIM = D_REC // REC_HEADS
REC_CONV_W = 4
RG_C = 8.0
CONF_CONV_W = 31
XA_HEADS = 4
XA_HEAD_DIM = D_MODEL // XA_HEADS
N_EXPERTS = 64
TOP_K = 8
N_GROUPS = 8
TOPK_GROUPS = 4
GROUP_SIZE = N_EXPERTS // N_GROUPS
D_EXPERT = D_MODEL // 4
ROUTED_SCALE = 2.5
LN_EPS = 1e-5
DEPTH = 1
ALPHA = (2 * DEPTH) ** 0.25

LANES = 128
SUBLANES = 8
VMEM_LIMIT = 56 * 1024 * 1024

MIX_TL = 64
SCAN_PAD = 8
ATT_TM = 512
ATT_G = 4
TOK_TM = 384
MOE_NSEG = 3
MOE_R = 256
ROW_SLAB = D_MODEL // LANES
TILE_S = MOE_R + 8


def _cp(sem=None):
    return pltpu.CompilerParams(dimension_semantics=sem, vmem_limit_bytes=VMEM_LIMIT)


def _ln(x, g, b):
    mu = jnp.mean(x, axis=-1, keepdims=True)
    xc = x - mu
    var = jnp.mean(xc * xc, axis=-1, keepdims=True)
    return xc * lax.rsqrt(var + LN_EPS) * g + b


def _sigmoid(x):
    return 1.0 / (1.0 + jnp.exp(-x))


def _silu(x):
    return x * _sigmoid(x)


def _softplus(x):
    return jnp.maximum(x, 0.0) + jnp.log1p(jnp.exp(-jnp.abs(x)))


def _gelu_tanh(x):
    c = 0.7978845608028654
    return 0.5 * x * (1.0 + jnp.tanh(c * (x + 0.044715 * (x * x * x))))


def _dot(a, b):
    return jnp.dot(a, b, preferred_element_type=F32)


def _full(shape):
    n = len(shape)
    return pl.BlockSpec(shape, lambda *_: (0,) * n)


def _rglru_coeffs(ucf, wg_ref, bg_ref, lam_ref):
    gates = _dot(ucf.astype(BF16), wg_ref[...]) + bg_ref[...]
    r = _sigmoid(gates[:, :D_REC])
    ig = _sigmoid(gates[:, D_REC:])
    log_a = (-RG_C) * r * _softplus(-lam_ref[...])
    a = jnp.exp(log_a)
    bt = jnp.sqrt(-jnp.tanh(log_a) * (a * a + 1.0)) * (ig * ucf)
    return a, bt


def _mix_prompt_kernel(x_ref, win_ref, bin_ref, wca_ref, bca_ref, wg_ref, bg_ref, lam_ref,
                       wcb_ref, bcb_ref, lcg_ref, lcb_ref, wout_ref, l1g_ref, l1b_ref,
                       x1_ref, hl_ref, ca_ref, cb_ref,
                       xs, exta, extb, hc, *, nb, tl):
    i = pl.program_id(0)
    m = nb * tl
    wa = REC_CONV_W
    wb = CONF_CONV_W
    ha = (wa - 1) * nb
    hb = (wb - 1) * nb
    nch = D_MODEL // LANES

    @pl.when(i == 0)
    def _():
        exta[0:ha, :] = jnp.zeros((ha, D_REC), F32)
        extb[0:hb, :] = jnp.zeros((hb, D_CONV), F32)
        hc[...] = jnp.zeros((nb, D_REC), F32)

    for n in range(nb):
        for c in range(nch):
            xs[c, pl.ds(n, tl, stride=nb), :] = x_ref[n, :, c * LANES:(c + 1) * LANES]
    xf = jnp.concatenate([xs[c] for c in range(nch)], axis=-1)
    z = _dot(xf.astype(BF16), win_ref[...]) + bin_ref[...]
    g = z[:, 0:D_REC]
    u = z[:, D_REC:2 * D_REC]
    v = z[:, 2 * D_REC:2 * D_REC + D_CONV]
    gg = z[:, 2 * D_REC + D_CONV:]

    exta[ha:ha + m, :] = u
    uc = jnp.zeros((m, D_REC), F32) + bca_ref[...]
    for j in range(wa):
        uc = uc + wca_ref[j:j + 1, :] * exta[j * nb:j * nb + m, :]
    tail_a = exta[m:m + ha, :]
    ca_ref[...] = tail_a.reshape(wa - 1, nb, D_REC)
    exta[0:ha, :] = tail_a

    a, bt = _rglru_coeffs(uc, wg_ref, bg_ref, lam_ref)

    h = hc[...]
    hs = []
    for t in range(tl):
        h = a[t * nb:(t + 1) * nb, :] * h + bt[t * nb:(t + 1) * nb, :]
        hs.append(h)
    hc[...] = h
    hl_ref[...] = h
    y_a = _gelu_tanh(g) * jnp.concatenate(hs, axis=0)

    extb[hb:hb + m, :] = v * _sigmoid(gg)
    c = jnp.zeros((m, D_CONV), F32) + bcb_ref[...]
    for j in range(wb):
        c = c + wcb_ref[j:j + 1, :] * extb[j * nb:j * nb + m, :]
    tail_b = extb[m:m + hb, :]
    cb_ref[...] = tail_b.reshape(wb - 1, nb, D_CONV)
    extb[0:hb, :] = tail_b
    y_b = _silu(_ln(c, lcg_ref[...], lcb_ref[...]))

    mix = jnp.concatenate([y_a, y_b], axis=-1).astype(BF16)
    o = _dot(mix, wout_ref[...])
    x1 = _ln(ALPHA * xf + o, l1g_ref[...], l1b_ref[...])
    for c in range(nch):
        xs[c] = x1[:, c * LANES:(c + 1) * LANES]
    for n in range(nb):
        for c in range(nch):
            x1_ref[n, :, c * LANES:(c + 1) * LANES] = xs[c, pl.ds(n, tl, stride=nb), :]


def _mix_prompt(x, wp):
    nb, L, _ = x.shape
    tl = MIX_TL
    m = nb * tl
    kern = functools.partial(_mix_prompt_kernel, nb=nb, tl=tl)
    ins = [wp['w_in'], wp['b_in'], wp['w_conv_a'], wp['b_conv_a'], wp['w_gate'], wp['b_gate'],
           wp['lam'], wp['w_conv_b'], wp['b_conv_b'], wp['ln_c_g'], wp['ln_c_b'], wp['w_out'],
           wp['ln1_g'], wp['ln1_b']]
    x1, h, ca, cb = pl.pallas_call(
        kern,
        grid=(L // tl,),
        in_specs=[pl.BlockSpec((nb, tl, D_MODEL), lambda i: (0, i, 0))] + [_full(a.shape) for a in ins],
        out_specs=[pl.BlockSpec((nb, tl, D_MODEL), lambda i: (0, i, 0)),
                   _full((nb, D_REC)), _full((REC_CONV_W - 1, nb, D_REC)),
                   _full((CONF_CONV_W - 1, nb, D_CONV))],
        out_shape=[jax.ShapeDtypeStruct((nb, L, D_MODEL), F32),
                   jax.ShapeDtypeStruct((nb, D_REC), F32),
                   jax.ShapeDtypeStruct((REC_CONV_W - 1, nb, D_REC), F32),
                   jax.ShapeDtypeStruct((CONF_CONV_W - 1, nb, D_CONV), F32)],
        scratch_shapes=[pltpu.VMEM((D_MODEL // LANES, m, LANES), F32),
                        pltpu.VMEM(((REC_CONV_W - 1) * nb + m, D_REC), F32),
                        pltpu.VMEM(((CONF_CONV_W - 1) * nb + m, D_CONV), F32),
                        pltpu.VMEM((nb, D_REC), F32)],
        compiler_params=_cp(("arbitrary",)),
        name="mix_prompt",
    )(x, *ins)
    return x1, h, jnp.transpose(ca, (1, 0, 2)), jnp.transpose(cb, (1, 0, 2))


def _mix_sample_kernel(x_ref, sta_ref, stb_ref, h0_ref, win_ref, bin_ref, wca_ref, bca_ref, wg_ref,
                       bg_ref, lam_ref, wcb_ref, bcb_ref, lcg_ref, lcb_ref, wout_ref, l1g_ref, l1b_ref,
                       x1_ref, h_ref, u_ref, glu_ref):
    wa = REC_CONV_W
    wb = CONF_CONV_W
    xf = x_ref[...]
    z = _dot(xf.astype(BF16), win_ref[...]) + bin_ref[...]
    g = z[:, 0:D_REC]
    u = z[:, D_REC:2 * D_REC]
    v = z[:, 2 * D_REC:2 * D_REC + D_CONV]
    gg = z[:, 2 * D_REC + D_CONV:]

    uc = bca_ref[...] + wca_ref[wa - 1:wa, :] * u
    for j in range(wa - 1):
        uc = uc + wca_ref[j:j + 1, :] * sta_ref[j]
    a, bt = _rglru_coeffs(uc, wg_ref, bg_ref, lam_ref)
    h = a * h0_ref[...] + bt
    y_a = _gelu_tanh(g) * h

    glu = v * _sigmoid(gg)
    c = bcb_ref[...] + wcb_ref[wb - 1:wb, :] * glu
    for j in range(wb - 1):
        c = c + wcb_ref[j:j + 1, :] * stb_ref[j]
    y_b = _silu(_ln(c, lcg_ref[...], lcb_ref[...]))

    mix = jnp.concatenate([y_a, y_b], axis=-1).astype(BF16)
    o = _dot(mix, wout_ref[...])
    x1_ref[...] = _ln(ALPHA * xf + o, l1g_ref[...], l1b_ref[...])
    h_ref[...] = h
    u_ref[...] = u
    glu_ref[...] = glu


def _mix_sample(x, sta, stb, h0, wp):
    n = x.shape[0]
    ins = [x, sta, stb, h0, wp['w_in'], wp['b_in'], wp['w_conv_a'], wp['b_conv_a'], wp['w_gate'],
           wp['b_gate'], wp['lam'], wp['w_conv_b'], wp['b_conv_b'], wp['ln_c_g'], wp['ln_c_b'],
           wp['w_out'], wp['ln1_g'], wp['ln1_b']]
    return pl.pallas_call(
        _mix_sample_kernel,
        grid=(1,),
        in_specs=[_full(a.shape) for a in ins],
        out_specs=[_full((n, D_MODEL)), _full((n, D_REC)), _full((n, D_REC)), _full((n, D_CONV))],
        out_shape=[jax.ShapeDtypeStruct((n, D_MODEL), F32), jax.ShapeDtypeStruct((n, D_REC), F32),
                   jax.ShapeDtypeStruct((n, D_REC), F32), jax.ShapeDtypeStruct((n, D_CONV), F32)],
        compiler_params=_cp(("arbitrary",)),
        name="mix_sample",
    )(*ins)


def _linear_kernel(a_ref, w_ref, o_ref):
    o_ref[...] = _dot(a_ref[...].astype(BF16), w_ref[...])


def _linear(a, w, tm, name):
    m, k = a.shape
    n = w.shape[1]
    return pl.pallas_call(
        _linear_kernel,
        grid=(m // tm,),
        in_specs=[pl.BlockSpec((tm, k), lambda i: (i, 0)), _full(w.shape)],
        out_specs=pl.BlockSpec((tm, n), lambda i: (i, 0)),
        out_shape=jax.ShapeDtypeStruct((m, n), F32),
        compiler_params=_cp(("parallel",)),
        name=name,
    )(a, w)


def _attn_out_sample_kernel(a_ref, r_ref, w_ref, g_ref, b_ref, rows_in, slab_in, o_ref, slab_ref):
    del rows_in, slab_in
    o = _dot(a_ref[...].astype(BF16), w_ref[...])
    _store_rows_and_slab(_ln(ALPHA * r_ref[...] + o, g_ref[...], b_ref[...]), o_ref, slab_ref)


def _attn_out_sample(a, res, w, g, b, x2_rows, x2_slab):
    m, k = a.shape
    n = w.shape[1]
    blk = x2_rows.shape[0] // m - 1
    assert (blk + 1) * m == x2_rows.shape[0]
    any_spec = pl.BlockSpec(memory_space=pl.ANY)
    return pl.pallas_call(
        _attn_out_sample_kernel,
        grid=(1,),
        in_specs=[_full(a.shape), _full(res.shape), _full(w.shape), _full(g.shape), _full(b.shape),
                  any_spec, any_spec],
        out_specs=[pl.BlockSpec((m, n), lambda i: (blk, 0)),
                   pl.BlockSpec((m * ROW_SLAB, LANES), lambda i: (blk, 0))],
        out_shape=[jax.ShapeDtypeStruct(x2_rows.shape, F32), jax.ShapeDtypeStruct(x2_slab.shape, F32)],
        input_output_aliases={5: 0, 6: 1},
        compiler_params=_cp(("arbitrary",)),
        name="attn_out_sample",
    )(a, res, w, g, b, x2_rows, x2_slab)


def _attn_prompt_kernel(x_ref, k_ref, v_ref, wq_ref, wo_ref, g_ref, b_ref, o_ref, slab_ref):
    x = x_ref[0]
    q = _dot(x.astype(BF16), wq_ref[...])
    k = k_ref[0].astype(BF16)
    v = v_ref[0].astype(BF16)
    scale = XA_HEAD_DIM ** -0.5
    outs = []
    for h in range(XA_HEADS):
        sl = slice(h * XA_HEAD_DIM, (h + 1) * XA_HEAD_DIM)
        s = lax.dot_general(q[:, sl].astype(BF16), k[:, sl], (((1,), (1,)), ((), ())),
                            preferred_element_type=F32) * scale
        e = jnp.exp(s - jnp.max(s, axis=-1, keepdims=True))
        p = e / jnp.sum(e, axis=-1, keepdims=True)
        outs.append(_dot(p.astype(BF16), v[:, sl]))
    o = jnp.concatenate(outs, axis=-1).astype(BF16)
    att = _dot(o, wo_ref[...])
    _store_rows_and_slab(_ln(ALPHA * x + att, g_ref[...], b_ref[...]), o_ref, slab_ref)


def _store_rows_and_slab(x2, o_ref, slab_ref):
    rows = x2.shape[0]
    o_ref[...] = x2
    for c in range(ROW_SLAB):
        slab_ref[pl.ds(c, rows, stride=ROW_SLAB), :] = x2[:, c * LANES:(c + 1) * LANES]


def _attn_prompt(x1, mk, mv, wq, wo, g, b, t_all):
    nb, L, d = x1.shape
    nm = mk.shape[1]
    tm = ATT_TM
    per = L // tm
    return pl.pallas_call(
        _attn_prompt_kernel,
        grid=(nb, per),
        in_specs=[pl.BlockSpec((1, tm, d), lambda bi, i: (bi, i, 0)),
                  pl.BlockSpec((1, nm, d), lambda bi, i: (bi, 0, 0)),
                  pl.BlockSpec((1, nm, d), lambda bi, i: (bi, 0, 0)),
                  _full(wq.shape), _full(wo.shape), _full(g.shape), _full(b.shape)],
        out_specs=[pl.BlockSpec((tm, d), lambda bi, i: (bi * per + i, 0)),
                   pl.BlockSpec((tm * ROW_SLAB, LANES), lambda bi, i: (bi * per + i, 0))],
        out_shape=[jax.ShapeDtypeStruct((t_all, d), F32),
                   jax.ShapeDtypeStruct((t_all * ROW_SLAB, LANES), F32)],
        compiler_params=_cp(("parallel", "parallel")),
        name="attn_prompt",
    )(x1, mk, mv, wq, wo, g, b)


def _attn_sample_kernel(q_ref, k_ref, v_ref, o_ref):
    scale = XA_HEAD_DIM ** -0.5
    for gi in range(ATT_G):
        qn = q_ref[gi]
        s = jnp.sum(k_ref[gi] * qn[None], axis=-1, keepdims=True) * scale
        e = jnp.exp(s - jnp.max(s, axis=0, keepdims=True))
        p = e / jnp.sum(e, axis=0, keepdims=True)
        o_ref[gi] = jnp.sum(p * v_ref[gi], axis=0)


def _attn_sample(q, ck, cv):
    _, n, nm, nh, dh = ck.shape
    q3 = q.reshape(n, nh, dh)
    cache_spec = pl.BlockSpec((None, ATT_G, nm, nh, dh), lambda i: (0, i, 0, 0, 0))
    out = pl.pallas_call(
        _attn_sample_kernel,
        grid=(n // ATT_G,),
        in_specs=[pl.BlockSpec((ATT_G, nh, dh), lambda i: (i, 0, 0)), cache_spec, cache_spec],
        out_specs=pl.BlockSpec((ATT_G, nh, dh), lambda i: (i, 0, 0)),
        out_shape=jax.ShapeDtypeStruct((n, nh, dh), F32),
        compiler_params=_cp(("parallel",)),
        name="attn_sample",
    )(q3, ck, cv)
    return out.reshape(n, nh * dh)


def _first_argmax(val, iota, n, axis):
    mx = jnp.max(val, axis=axis, keepdims=True)
    idx = jnp.min(jnp.where(val == mx, iota, n), axis=axis, keepdims=True)
    return mx, idx


def _router_kernel(x_ref, wr_ref, rb_ref, idx_ref, gate_ref, rank_ref, ps_ref, be_ref, bn_ref, carry):
    tm = x_ref.shape[0]
    neg = -jnp.inf

    @pl.when(pl.program_id(1) == 0)
    def _():
        carry[...] = jnp.zeros(carry.shape, F32)

    logits = lax.dot_general(wr_ref[...], x_ref[...], (((1,), (1,)), ((), ())),
                             precision=lax.Precision.HIGHEST, preferred_element_type=F32)
    scores = _sigmoid(logits)
    biased = scores + rb_ref[...]
    grp = biased.reshape(N_GROUPS, GROUP_SIZE, tm)
    io_g = lax.broadcasted_iota(jnp.int32, (N_GROUPS, GROUP_SIZE, tm), 1)
    m1, i1 = _first_argmax(grp, io_g, GROUP_SIZE, 1)
    m2 = jnp.max(jnp.where(io_g == i1, neg, grp), axis=1, keepdims=True)
    gscore = (m1 + m2).reshape(N_GROUPS, tm)
    io_n = lax.broadcasted_iota(jnp.int32, (N_GROUPS, tm), 0)
    gsel = jnp.zeros((N_GROUPS, tm), jnp.bool_)
    for _ in range(TOPK_GROUPS):
        _, gi = _first_argmax(gscore, io_n, N_GROUPS, 0)
        hit = io_n == gi
        gsel = jnp.logical_or(gsel, hit)
        gscore = jnp.where(hit, neg, gscore)
    emask = jnp.broadcast_to(gsel.reshape(N_GROUPS, 1, tm), (N_GROUPS, GROUP_SIZE, tm)).reshape(N_EXPERTS, tm)
    val = jnp.where(emask, biased, neg)
    io_e = lax.broadcasted_iota(jnp.int32, (N_EXPERTS, tm), 0)
    idxs, ws, hits = [], [], []
    for _ in range(TOP_K):
        _, ei = _first_argmax(val, io_e, N_EXPERTS, 0)
        hit = io_e == ei
        ws.append(jnp.sum(jnp.where(hit, scores, 0.0), axis=0, keepdims=True))
        idxs.append(ei)
        hits.append(hit)
        val = jnp.where(hit, neg, val)
    w = jnp.concatenate(ws, axis=0)
    idx_ref[...] = jnp.concatenate(idxs, axis=0)
    gate_ref[...] = w / jnp.sum(w, axis=0, keepdims=True) * ROUTED_SCALE

    onehot = jnp.zeros((N_EXPERTS, tm), F32)
    for hit in hits:
        onehot = onehot + hit.astype(F32)
    tri = (lax.broadcasted_iota(jnp.int32, (tm, tm), 0) <= lax.broadcasted_iota(jnp.int32, (tm, tm), 1))
    csum = _dot(onehot.astype(BF16), tri.astype(BF16)) + carry[...]
    ranks = [jnp.sum(jnp.where(hit, csum, 0.0), axis=0, keepdims=True) - 1.0 for hit in hits]
    rank_ref[...] = jnp.concatenate(ranks, axis=0).astype(jnp.int32)
    counts = csum[:, tm - 1:tm]
    carry[...] = counts

    @pl.when(pl.program_id(1) == pl.num_programs(1) - 1)
    def _():
        r = float(MOE_R)
        padded = jnp.floor((counts + (r - 1.0)) / r) * r
        below = (lax.broadcasted_iota(jnp.int32, (N_EXPERTS, N_EXPERTS), 1)
                 < lax.broadcasted_iota(jnp.int32, (N_EXPERTS, N_EXPERTS), 0)).astype(F32)
        pstart = jnp.dot(below, jnp.broadcast_to(padded, (N_EXPERTS, LANES)), precision=lax.Precision.HIGHEST,
                         preferred_element_type=F32)[:, 0:1]
        pend = pstart + padded
        nb = be_ref.shape[-1]
        bpos = lax.broadcasted_iota(jnp.int32, (1, nb), 1).astype(F32) * r
        be = jnp.minimum(jnp.sum((bpos >= pend).astype(F32), axis=0, keepdims=True), float(N_EXPERTS - 1))
        sel = lax.broadcasted_iota(jnp.int32, (N_EXPERTS, nb), 0) == be.astype(jnp.int32)
        cnt_b = jnp.sum(jnp.where(sel, counts, 0.0), axis=0, keepdims=True)
        st_b = jnp.sum(jnp.where(sel, pstart, 0.0), axis=0, keepdims=True)
        ps_ref[0] = pstart.astype(jnp.int32)
        be_ref[0] = be.astype(jnp.int32)
        bn_ref[0] = jnp.clip(cnt_b - (bpos - st_b), 0.0, r).astype(jnp.int32)


def _router(x2, wr_t, rb, nseg, nblk_pad):
    t, d = x2.shape
    tm = LANES
    tiles = t // (nseg * tm)
    tok_spec = pl.BlockSpec((TOP_K, tm), lambda s, j: (s * tiles + j, 0))
    tab_spec = pl.BlockSpec((1, 1, nblk_pad), lambda s, j: (s, 0, 0))
    tok_shape = (t // tm * TOP_K, tm)
    return pl.pallas_call(
        _router_kernel,
        grid=(nseg, tiles),
        in_specs=[pl.BlockSpec((tm, d), lambda s, j: (s * tiles + j, 0)), _full(wr_t.shape), _full(rb.shape)],
        out_specs=[tok_spec, tok_spec, tok_spec, pl.BlockSpec((1, N_EXPERTS, 1), lambda s, j: (s, 0, 0)),
                   tab_spec, tab_spec],
        out_shape=[jax.ShapeDtypeStruct(tok_shape, jnp.int32), jax.ShapeDtypeStruct(tok_shape, F32),
                   jax.ShapeDtypeStruct(tok_shape, jnp.int32),
                   jax.ShapeDtypeStruct((nseg, N_EXPERTS, 1), jnp.int32),
                   jax.ShapeDtypeStruct((nseg, 1, nblk_pad), jnp.int32),
                   jax.ShapeDtypeStruct((nseg, 1, nblk_pad), jnp.int32)],
        scratch_shapes=[pltpu.VMEM((N_EXPERTS, 1), F32)],
        compiler_params=_cp(("arbitrary", "arbitrary")),
        name="router",
    )(x2, wr_t, rb)


def _plan_sc(idx_f, rank_f, gate_f, pstart_f, nseg, tseg, nslots, lead):
    words = tseg * TOP_K
    chunk = words // 4
    assert chunk * 4 == words and chunk % 16 == 0 and nslots % 16 == 0
    info = plsc.get_sparse_core_info()
    ncores = info.num_cores
    nl = info.num_lanes
    mesh = plsc.VectorSubcoreMesh(core_axis_name="c", subcore_axis_name="s")
    dummy_off = tseg * ROW_SLAB

    def body(idx_hbm, rank_hbm, gate_hbm, ps_hbm, *rest):
        outs = rest[:3 * nseg]
        idx_v, rank_v, val_v, ps_v, slot_v = rest[3 * nseg:]
        wid = lax.axis_index("s") * ncores + lax.axis_index("c")
        lane = lax.iota(jnp.int32, nl)

        def run(seg, kind):
            want_gate = kind == 2
            out_hbm = outs[3 * seg + kind]
            pltpu.sync_copy(ps_hbm.at[pl.ds(seg * N_EXPERTS, N_EXPERTS)], ps_v)
            fill = jnp.full((nl,), dummy_off if kind == 1 else 0, jnp.int32)

            @pl.loop(0, nslots, step=nl)
            def _(i):
                slot_v[pl.ds(i, nl)] = fill

            @pl.loop(0, 4)
            def _(ci):
                base = seg * words + ci * chunk
                pltpu.sync_copy(idx_hbm.at[pl.ds(base, chunk)], idx_v)
                pltpu.sync_copy(rank_hbm.at[pl.ds(base, chunk)], rank_v)
                if want_gate:
                    pltpu.sync_copy(gate_hbm.at[pl.ds(base, chunk)], val_v)

                @pl.loop(0, chunk, step=nl)
                def _(o):
                    e = idx_v[pl.ds(o, nl)]
                    d = plsc.load_gather(ps_v, [e]) + rank_v[pl.ds(o, nl)] + lead
                    if want_gate:
                        plsc.store_scatter(slot_v, [d], val_v[pl.ds(o, nl)])
                    else:
                        f = ci * chunk + o + lane
                        tok = ((f >> 10) << 7) | (f & (LANES - 1))
                        plsc.store_scatter(slot_v, [d], tok * ROW_SLAB)

            pltpu.sync_copy(slot_v, out_hbm)

        for seg in range(nseg):
            for kind in range(3):
                @pl.when(wid == 3 * seg + kind)
                def _(seg=seg, kind=kind):
                    run(seg, kind)

    out_type = [jax.ShapeDtypeStruct((nslots,), jnp.int32) for _ in range(3 * nseg)]
    fn = pl.kernel(
        body, out_type=out_type, mesh=mesh,
        scratch_types=[pltpu.VMEM((chunk,), jnp.int32), pltpu.VMEM((chunk,), jnp.int32),
                       pltpu.VMEM((chunk,), jnp.int32), pltpu.VMEM((N_EXPERTS,), jnp.int32),
                       pltpu.VMEM((nslots,), jnp.int32)],
        compiler_params=pltpu.CompilerParams(needs_layout_passes=False), name="plan_sc")
    return fn(idx_f, rank_f, lax.bitcast_convert_type(gate_f, jnp.int32), pstart_f)


def _expert_kernel(be_ref, bn_ref, slotg_ref, slots_ref, x_ref, ge_ref, go_ref,
                   w1e_ref, w3e_ref, w2e_ref, w1o_ref, w3o_ref, w2o_ref, *rest):
    acc_ref, tile_a, tile_b, ys_a, ys_b = rest[-5:]
    _expert_body(bn_ref, slotg_ref, slots_ref, x_ref, (ge_ref, w1e_ref, w3e_ref, w2e_ref),
                 (go_ref, w1o_ref, w3o_ref, w2o_ref), acc_ref, tile_a, tile_b, ys_a, ys_b)
    del be_ref


def _expert_body(bn_ref, slotg_ref, slots_ref, x_ref, even_refs, odd_refs,
                 acc_ref, tile_a, tile_b, ys_a, ys_b):
    i = pl.program_id(0)
    r = MOE_R
    s = TILE_S
    zero_rows = 64 * ROW_SLAB

    def gather(blk, tile):
        base = (blk + 1) * r
        for row in range(r):
            off = pl.multiple_of(slotg_ref[base + row], ROW_SLAB)
            tile[pl.ds(row, ROW_SLAB, stride=s), :] = x_ref[pl.ds(off, ROW_SLAB), :]

    def ffn(tile, refs, ys):
        g_ref, w1_ref, w3_ref, w2_ref = refs
        xb = jnp.concatenate([tile[c * s:c * s + r, :] for c in range(ROW_SLAB)], axis=-1).astype(BF16)
        gcol = jnp.transpose(jnp.broadcast_to(g_ref[0], (SUBLANES, r)))[:, 0:1]
        hh = (_silu(_dot(xb, w1_ref[0])) * _dot(xb, w3_ref[0])).astype(BF16)
        y = _dot(hh, w2_ref[0]) * gcol
        for g in range(r // SUBLANES):
            for c in range(ROW_SLAB):
                ys[pl.ds((g * ROW_SLAB + c) * SUBLANES, SUBLANES), :] = (
                    y[g * SUBLANES:(g + 1) * SUBLANES, c * LANES:(c + 1) * LANES])

    def scatter(blk, ys):
        base = (blk + 1) * r
        for q in range(r // 4):
            offs, vals = [], []
            for row in range(q * 4, q * 4 + 4):
                off = pl.multiple_of(slots_ref[base + row], ROW_SLAB)
                src = (row // SUBLANES) * (ROW_SLAB * SUBLANES) + row % SUBLANES
                vals.append(acc_ref[pl.ds(off, ROW_SLAB), :] + ys[pl.ds(src, ROW_SLAB, stride=SUBLANES), :])
                offs.append(off)
            for off, val in zip(offs, vals):
                acc_ref[pl.ds(off, ROW_SLAB), :] = val

    @pl.when(i == 0)
    def _():
        def zero(k, c):
            acc_ref[pl.ds(pl.multiple_of(k * zero_rows, zero_rows), zero_rows), :] = jnp.zeros((zero_rows, LANES), F32)
            return c

        lax.fori_loop(0, acc_ref.shape[0] // zero_rows, zero, 0)
        rem = acc_ref.shape[0] % zero_rows
        if rem:
            acc_ref[pl.ds(acc_ref.shape[0] - rem, rem), :] = jnp.zeros((rem, LANES), F32)
        tile_a[...] = jnp.zeros(tile_a.shape, F32)
        tile_b[...] = jnp.zeros(tile_b.shape, F32)
        ys_b[...] = jnp.zeros(ys_b.shape, F32)
        gather(0, tile_a)

    even = 2 * i
    odd = even + 1
    pending = bn_ref[even] + bn_ref[odd] + bn_ref[jnp.maximum(even - 1, 0)]

    @pl.when(pending > 0)
    def _():
        gather(odd, tile_b)
        ffn(tile_a, even_refs, ys_a)
        scatter(even - 1, ys_b)
        gather(even + 2, tile_a)
        ffn(tile_b, odd_refs, ys_b)
        scatter(even, ys_a)


def _expert_segment(si, nseg, blk_e, blk_n, slots_g, slots_s, gates3, xslab3, w1, w3, w2, tseg, nblk,
                    routed_prev):
    r = MOE_R
    acc_rows = (tseg + SUBLANES) * ROW_SLAB
    one = pl.Buffered(1)
    w1_blk = (1, D_MODEL, D_EXPERT)
    w2_blk = (1, D_EXPERT, D_MODEL)
    g_blk = (1, 1, r)
    in_specs = [pl.BlockSpec((None, tseg * ROW_SLAB, LANES), lambda i, *_: (si, 0, 0), pipeline_mode=one),
                pl.BlockSpec(g_blk, lambda i, *_: (2 * i + 1, 0, 0)),
                pl.BlockSpec(g_blk, lambda i, *_: (2 * i + 2, 0, 0)),
                pl.BlockSpec(w1_blk, lambda i, be, *_: (be[2 * i], 0, 0)),
                pl.BlockSpec(w1_blk, lambda i, be, *_: (be[2 * i], 0, 0)),
                pl.BlockSpec(w2_blk, lambda i, be, *_: (be[2 * i], 0, 0)),
                pl.BlockSpec(w1_blk, lambda i, be, *_: (be[2 * i + 1], 0, 0)),
                pl.BlockSpec(w1_blk, lambda i, be, *_: (be[2 * i + 1], 0, 0)),
                pl.BlockSpec(w2_blk, lambda i, be, *_: (be[2 * i + 1], 0, 0))]
    args = [blk_e, blk_n, slots_g, slots_s, xslab3, gates3, gates3, w1, w3, w2, w1, w3, w2]
    aliases = {}
    if routed_prev is not None:
        in_specs.append(pl.BlockSpec(memory_space=pl.ANY))
        aliases = {len(args): 0}
        args.append(routed_prev)
    grid_spec = pltpu.PrefetchScalarGridSpec(
        num_scalar_prefetch=4,
        grid=(nblk // 2 + 1,),
        in_specs=in_specs,
        out_specs=pl.BlockSpec((None, acc_rows, LANES), lambda i, *_: (si, 0, 0), pipeline_mode=one),
        scratch_shapes=[pltpu.VMEM((ROW_SLAB * TILE_S, LANES), F32),
                        pltpu.VMEM((ROW_SLAB * TILE_S, LANES), F32),
                        pltpu.VMEM((r * ROW_SLAB, LANES), F32),
                        pltpu.VMEM((r * ROW_SLAB, LANES), F32)],
    )
    return pl.pallas_call(
        _expert_kernel,
        grid_spec=grid_spec,
        out_shape=jax.ShapeDtypeStruct((nseg, acc_rows, LANES), F32),
        input_output_aliases=aliases,
        compiler_params=_cp(("arbitrary",)),
        name="experts",
    )(*args)


def _final_kernel(x_ref, r_ref, w13_ref, w2_ref, g_ref, b_ref, yp_ref, ys_ref, *, tiles, p_tiles):
    tm = x_ref.shape[0]
    x = x_ref[...]
    h13 = _dot(x.astype(BF16), w13_ref[...])
    hh = (_silu(h13[:, :D_EXPERT]) * h13[:, D_EXPERT:]).astype(BF16)
    sh = _dot(hh, w2_ref[...])
    routed = jnp.concatenate([r_ref[pl.ds(c, tm, stride=ROW_SLAB), :] for c in range(ROW_SLAB)], axis=-1)
    y = _ln(ALPHA * x + (sh + routed), g_ref[...], b_ref[...])
    tile = pl.program_id(0) * tiles + pl.program_id(1)

    @pl.when(tile < p_tiles)
    def _():
        yp_ref[...] = y

    @pl.when(tile >= p_tiles)
    def _():
        ys_ref[...] = y


def _final(x2, routed3, w13, w2, g, b, n_prompt):
    t, d = x2.shape
    nseg = routed3.shape[0]
    tm = LANES
    tiles = t // (nseg * tm)
    p_tiles = n_prompt // tm
    n_sample = t - n_prompt
    assert p_tiles * tm == n_prompt and n_sample == tm
    return pl.pallas_call(
        functools.partial(_final_kernel, tiles=tiles, p_tiles=p_tiles),
        grid=(nseg, tiles),
        in_specs=[pl.BlockSpec((tm, d), lambda s, j: (s * tiles + j, 0)),
                  pl.BlockSpec((None, tm * ROW_SLAB, LANES), lambda s, j: (s, j, 0)),
                  _full(w13.shape), _full(w2.shape), _full(g.shape), _full(b.shape)],
        out_specs=[pl.BlockSpec((tm, d), lambda s, j: (jnp.minimum(s * tiles + j, p_tiles - 1), 0)),
                   pl.BlockSpec((tm, d), lambda s, j: (0, 0))],
        out_shape=[jax.ShapeDtypeStruct((n_prompt, d), F32), jax.ShapeDtypeStruct((n_sample, d), F32)],
        compiler_params=_cp(("arbitrary", "arbitrary")),
        name="final",
    )(x2, routed3, w13, w2, g, b)


def _block_diag_dense(w):
    h, dh, _ = w.shape
    eye = jnp.eye(h, dtype=w.dtype)
    return jnp.einsum('hij,hg->higj', w, eye).reshape(h * dh, h * dh)


def kernel(x_prompt, x_sample, mem_prompt, state_rglru_h, state_rglru_conv, state_dwconv, cache_mem_k, cache_mem_v, w_in, b_in, w_conv_a, b_conv_a, w_rg_a, b_rg_a, w_rg_x, b_rg_x, rg_lambda, w_conv_b, b_conv_b, ln_c_g, ln_c_b, w_out, ln1_g, ln1_b, w_q, w_k, w_v, w_o, ln2_g, ln2_b, w_router, router_bias, w1_e, w3_e, w2_e, w1_s, w3_s, w2_s, ln3_g, ln3_b):
    assert w_in.shape[0] == 1, "single-layer trunk"
    nb, L, d = x_prompt.shape
    ns = x_sample.shape[0]
    nm = mem_prompt.shape[1]
    row = lambda a: a.reshape(1, -1)

    wp = dict(
        w_in=w_in[0].astype(BF16), b_in=row(b_in[0]),
        w_conv_a=w_conv_a[0], b_conv_a=row(b_conv_a[0]),
        w_gate=jnp.concatenate([_block_diag_dense(w_rg_a[0]), _block_diag_dense(w_rg_x[0])], axis=1).astype(BF16),
        b_gate=row(jnp.concatenate([b_rg_a[0], b_rg_x[0]])),
        lam=row(rg_lambda[0]),
        w_conv_b=w_conv_b[0], b_conv_b=row(b_conv_b[0]),
        ln_c_g=row(ln_c_g[0]), ln_c_b=row(ln_c_b[0]),
        w_out=w_out[0].astype(BF16), ln1_g=row(ln1_g[0]), ln1_b=row(ln1_b[0]),
    )
    wq = w_q[0].astype(BF16)
    wo = w_o[0].astype(BF16)
    wkv = jnp.concatenate([w_k[0], w_v[0]], axis=1).astype(BF16)
    l2g, l2b = row(ln2_g[0]), row(ln2_b[0])

    x1_p, h_p, ca_p, cb_p = _mix_prompt(x_prompt, wp)
    kv = _linear(mem_prompt.reshape(nb * nm, d), wkv, 512, "kv_proj")
    mk_p = kv[:, :d].reshape(nb, nm, d)
    mv_p = kv[:, d:].reshape(nb, nm, d)
    t = nb * L + ns
    x2, x2_slab = _attn_prompt(x1_p, mk_p, mv_p, wq, wo, l2g, l2b, t)

    xs = x_sample.reshape(ns, d)
    sta = jnp.transpose(state_rglru_conv[0], (1, 0, 2))
    stb = jnp.transpose(state_dwconv[0], (1, 0, 2))
    x1_s, h_s, u_s, glu_s = _mix_sample(xs, sta, stb, state_rglru_h[0], wp)
    q_s = _linear(x1_s, wq, ns, "q_sample")
    o_s = _attn_sample(q_s, cache_mem_k, cache_mem_v)
    x2, x2_slab = _attn_out_sample(o_s, x1_s, wo, l2g, l2b, x2, x2_slab)

    nseg = MOE_NSEG
    tseg = t // nseg
    assert tseg * nseg == t and tseg % LANES == 0 and t % TOK_TM == 0
    nblk = -(-(tseg * TOP_K + N_EXPERTS * (MOE_R - 1)) // MOE_R)
    nblk += nblk % 2
    nblk_pad = -(-(nblk + 2) // LANES) * LANES
    idx_f, gate_f, rank_f, pstart, blk_e, blk_n = _router(
        x2, jnp.transpose(w_router[0]), router_bias[0].reshape(N_EXPERTS, 1), nseg, nblk_pad)
    plan = _plan_sc(idx_f.reshape(-1), rank_f.reshape(-1), gate_f.reshape(-1), pstart.reshape(-1),
                    nseg, tseg, (nblk + 4) * MOE_R, MOE_R)
    w1_eb, w3_eb, w2_eb = w1_e[0].astype(BF16), w3_e[0].astype(BF16), w2_e[0].astype(BF16)
    xslab3 = x2_slab.reshape(nseg, tseg * ROW_SLAB, LANES)
    routed3 = None
    for si in range(nseg):
        gates3 = lax.bitcast_convert_type(plan[3 * si + 2], F32).reshape(nblk + 4, 1, MOE_R)
        routed3 = _expert_segment(si, nseg, blk_e[si, 0], blk_n[si, 0], plan[3 * si], plan[3 * si + 1], gates3,
                                  xslab3, w1_eb, w3_eb, w2_eb, tseg, nblk, routed3)
    w13_s = jnp.concatenate([w1_s[0], w3_s[0]], axis=-1).astype(BF16)
    y_p, y_s = _final(x2, routed3, w13_s, w2_s[0].astype(BF16), row(ln3_g[0]), row(ln3_b[0]), nb * L)
    y_p = y_p.reshape(nb, L, d)
    y_s = y_s.reshape(ns, 1, d)
    new_ca_s = jnp.concatenate([state_rglru_conv[0][:, 1:], u_s[:, None, :]], axis=1)
    new_cb_s = jnp.concatenate([state_dwconv[0][:, 1:], glu_s[:, None, :]], axis=1)
    hd = XA_HEAD_DIM
    return (y_p, y_s, h_p[None], ca_p[None], cb_p[None],
            mk_p.reshape(1, nb, nm, XA_HEADS, hd), mv_p.reshape(1, nb, nm, XA_HEADS, hd),
            h_s[None], new_ca_s[None], new_cb_s[None])
```

```python
import functools

import jax
import jax.numpy as jnp
from jax import lax
from jax.experimental import pallas as pl
from jax.experimental.pallas import tpu as pltpu
from jax.experimental.pallas import tpu_sc as plsc

F32 = jnp.float32
BF16 = jnp.bfloat16

D_MODEL = 1024
D_REC = 512
D_CONV = 512
REC_HEADS = 8
REC_HEAD_DIM = D_REC // REC_HEADS
REC_CONV_W = 4
RG_C = 8.0
CONF_CONV_W = 31
XA_HEADS = 4
XA_HEAD_DIM = D_MODEL // XA_HEADS
N_EXPERTS = 64
TOP_K = 8
N_GROUPS = 8
TOPK_GROUPS = 4
GROUP_SIZE = N_EXPERTS // N_GROUPS
D_EXPERT = D_MODEL // 4
ROUTED_SCALE = 2.5
LN_EPS = 1e-5
DEPTH = 1
ALPHA = (2 * DEPTH) ** 0.25

LANES = 128
SUBLANES = 8
VMEM_LIMIT = 58 * 1024 * 1024

MIX_TL = 64
ATT_TM = 512
ATT_G = 4
TOK_TM = 512
MOE_NSEG = 3
MOE_R = 256
ROW_SLAB = D_MODEL // LANES
TILE_S = MOE_R + 8


def _cp(sem=None):
    return pltpu.CompilerParams(dimension_semantics=sem, vmem_limit_bytes=VMEM_LIMIT)


def _ln(x, g, b):
    mu = jnp.mean(x, axis=-1, keepdims=True)
    xc = x - mu
    var = jnp.mean(xc * xc, axis=-1, keepdims=True)
    return xc * lax.rsqrt(var + LN_EPS) * g + b


def _sigmoid(x):
    return 1.0 / (1.0 + jnp.exp(-x))


def _silu(x):
    return x * _sigmoid(x)


def _softplus(x):
    return jnp.maximum(x, 0.0) + jnp.log1p(jnp.exp(-jnp.abs(x)))


def _gelu_tanh(x):
    c = 0.7978845608028654
    return 0.5 * x * (1.0 + jnp.tanh(c * (x + 0.044715 * (x * x * x))))


def _dot(a, b):
    return jnp.dot(a, b, preferred_element_type=F32)


def _full(shape):
    n = len(shape)
    return pl.BlockSpec(shape, lambda *_: (0,) * n)


def _rglru_coeffs(ucf, wg_ref, bg_ref, lam_ref):
    gates = _dot(ucf.astype(BF16), wg_ref[...]) + bg_ref[...]
    r = _sigmoid(gates[:, :D_REC])
    ig = _sigmoid(gates[:, D_REC:])
    log_a = (-RG_C) * r * _softplus(-lam_ref[...])
    a = jnp.exp(log_a)
    bt = jnp.sqrt(-jnp.tanh(log_a) * (a * a + 1.0)) * (ig * ucf)
    return a, bt


def _mix_prompt_kernel(x_ref, win_ref, bin_ref, wca_ref, bca_ref, wg_ref, bg_ref, lam_ref,
                       wcb_ref, bcb_ref, lcg_ref, lcb_ref, wout_ref, l1g_ref, l1b_ref,
                       x1_ref, hl_ref, ca_ref, cb_ref,
                       xs, exta, extb, hc, *, nb, tl):
    i = pl.program_id(0)
    m = nb * tl
    wa = REC_CONV_W
    wb = CONF_CONV_W
    ha = (wa - 1) * nb
    hb = (wb - 1) * nb
    nch = D_MODEL // LANES

    @pl.when(i == 0)
    def _():
        exta[0:ha, :] = jnp.zeros((ha, D_REC), F32)
        extb[0:hb, :] = jnp.zeros((hb, D_CONV), F32)
        hc[...] = jnp.zeros((nb, D_REC), F32)

    for n in range(nb):
        for c in range(nch):
            xs[c, pl.ds(n, tl, stride=nb), :] = x_ref[n, :, c * LANES:(c + 1) * LANES]
    xf = jnp.concatenate([xs[c] for c in range(nch)], axis=-1)
    z = _dot(xf.astype(BF16), win_ref[...]) + bin_ref[...]
    g = z[:, 0:D_REC]
    u = z[:, D_REC:2 * D_REC]
    v = z[:, 2 * D_REC:2 * D_REC + D_CONV]
    gg = z[:, 2 * D_REC + D_CONV:]

    exta[ha:ha + m, :] = u
    uc = jnp.zeros((m, D_REC), F32) + bca_ref[...]
    for j in range(wa):
        uc = uc + wca_ref[j:j + 1, :] * exta[j * nb:j * nb + m, :]
    tail_a = exta[m:m + ha, :]
    ca_ref[...] = tail_a.reshape(wa - 1, nb, D_REC)
    exta[0:ha, :] = tail_a

    a, bt = _rglru_coeffs(uc, wg_ref, bg_ref, lam_ref)

    h = hc[...]
    hs = []
    for t in range(tl):
        h = a[t * nb:(t + 1) * nb, :] * h + bt[t * nb:(t + 1) * nb, :]
        hs.append(h)
    hc[...] = h
    hl_ref[...] = h
    y_a = _gelu_tanh(g) * jnp.concatenate(hs, axis=0)

    extb[hb:hb + m, :] = v * _sigmoid(gg)
    c = jnp.zeros((m, D_CONV), F32) + bcb_ref[...]
    for j in range(wb):
        c = c + wcb_ref[j:j + 1, :] * extb[j * nb:j * nb + m, :]
    tail_b = extb[m:m + hb, :]
    cb_ref[...] = tail_b.reshape(wb - 1, nb, D_CONV)
    extb[0:hb, :] = tail_b
    y_b = _silu(_ln(c, lcg_ref[...], lcb_ref[...]))

    mix = jnp.concatenate([y_a, y_b], axis=-1).astype(BF16)
    o = _dot(mix, wout_ref[...])
    x1 = _ln(ALPHA * xf + o, l1g_ref[...], l1b_ref[...])
    for c in range(nch):
        xs[c] = x1[:, c * LANES:(c + 1) * LANES]
    for n in range(nb):
        for c in range(nch):
            x1_ref[n, :, c * LANES:(c + 1) * LANES] = xs[c, pl.ds(n, tl, stride=nb), :]


def _mix_prompt(x, wp):
    nb, L, _ = x.shape
    tl = MIX_TL
    m = nb * tl
    kern = functools.partial(_mix_prompt_kernel, nb=nb, tl=tl)
    ins = [wp['w_in'], wp['b_in'], wp['w_conv_a'], wp['b_conv_a'], wp['w_gate'], wp['b_gate'],
           wp['lam'], wp['w_conv_b'], wp['b_conv_b'], wp['ln_c_g'], wp['ln_c_b'], wp['w_out'],
           wp['ln1_g'], wp['ln1_b']]
    x1, h, ca, cb = pl.pallas_call(
        kern,
        grid=(L // tl,),
        in_specs=[pl.BlockSpec((nb, tl, D_MODEL), lambda i: (0, i, 0))] + [_full(a.shape) for a in ins],
        out_specs=[pl.BlockSpec((nb, tl, D_MODEL), lambda i: (0, i, 0)),
                   _full((nb, D_REC)), _full((REC_CONV_W - 1, nb, D_REC)),
                   _full((CONF_CONV_W - 1, nb, D_CONV))],
        out_shape=[jax.ShapeDtypeStruct((nb, L, D_MODEL), F32),
                   jax.ShapeDtypeStruct((nb, D_REC), F32),
                   jax.ShapeDtypeStruct((REC_CONV_W - 1, nb, D_REC), F32),
                   jax.ShapeDtypeStruct((CONF_CONV_W - 1, nb, D_CONV), F32)],
        scratch_shapes=[pltpu.VMEM((D_MODEL // LANES, m, LANES), F32),
                        pltpu.VMEM(((REC_CONV_W - 1) * nb + m, D_REC), F32),
                        pltpu.VMEM(((CONF_CONV_W - 1) * nb + m, D_CONV), F32),
                        pltpu.VMEM((nb, D_REC), F32)],
        compiler_params=_cp(("arbitrary",)),
        name="mix_prompt",
    )(x, *ins)
    return x1, h, jnp.transpose(ca, (1, 0, 2)), jnp.transpose(cb, (1, 0, 2))


def _mix_sample_kernel(x_ref, sta_ref, stb_ref, h0_ref, win_ref, bin_ref, wca_ref, bca_ref, wg_ref,
                       bg_ref, lam_ref, wcb_ref, bcb_ref, lcg_ref, lcb_ref, wout_ref, l1g_ref, l1b_ref,
                       x1_ref, h_ref, u_ref, glu_ref):
    wa = REC_CONV_W
    wb = CONF_CONV_W
    xf = x_ref[...]
    z = _dot(xf.astype(BF16), win_ref[...]) + bin_ref[...]
    g = z[:, 0:D_REC]
    u = z[:, D_REC:2 * D_REC]
    v = z[:, 2 * D_REC:2 * D_REC + D_CONV]
    gg = z[:, 2 * D_REC + D_CONV:]

    uc = bca_ref[...] + wca_ref[wa - 1:wa, :] * u
    for j in range(wa - 1):
        uc = uc + wca_ref[j:j + 1, :] * sta_ref[j]
    a, bt = _rglru_coeffs(uc, wg_ref, bg_ref, lam_ref)
    h = a * h0_ref[...] + bt
    y_a = _gelu_tanh(g) * h

    glu = v * _sigmoid(gg)
    c = bcb_ref[...] + wcb_ref[wb - 1:wb, :] * glu
    for j in range(wb - 1):
        c = c + wcb_ref[j:j + 1, :] * stb_ref[j]
    y_b = _silu(_ln(c, lcg_ref[...], lcb_ref[...]))

    mix = jnp.concatenate([y_a, y_b], axis=-1).astype(BF16)
    o = _dot(mix, wout_ref[...])
    x1_ref[...] = _ln(ALPHA * xf + o, l1g_ref[...], l1b_ref[...])
    h_ref[...] = h
    u_ref[...] = u
    glu_ref[...] = glu


def _mix_sample(x, sta, stb, h0, wp):
    n = x.shape[0]
    ins = [x, sta, stb, h0, wp['w_in'], wp['b_in'], wp['w_conv_a'], wp['b_conv_a'], wp['w_gate'],
           wp['b_gate'], wp['lam'], wp['w_conv_b'], wp['b_conv_b'], wp['ln_c_g'], wp['ln_c_b'],
           wp['w_out'], wp['ln1_g'], wp['ln1_b']]
    return pl.pallas_call(
        _mix_sample_kernel,
        grid=(1,),
        in_specs=[_full(a.shape) for a in ins],
        out_specs=[_full((n, D_MODEL)), _full((n, D_REC)), _full((n, D_REC)), _full((n, D_CONV))],
        out_shape=[jax.ShapeDtypeStruct((n, D_MODEL), F32), jax.ShapeDtypeStruct((n, D_REC), F32),
                   jax.ShapeDtypeStruct((n, D_REC), F32), jax.ShapeDtypeStruct((n, D_CONV), F32)],
        compiler_params=_cp(("arbitrary",)),
        name="mix_sample",
    )(*ins)


def _linear_kernel(a_ref, w_ref, o_ref):
    o_ref[...] = _dot(a_ref[...].astype(BF16), w_ref[...])


def _linear(a, w, tm, name):
    m, k = a.shape
    n = w.shape[1]
    return pl.pallas_call(
        _linear_kernel,
        grid=(m // tm,),
        in_specs=[pl.BlockSpec((tm, k), lambda i: (i, 0)), _full(w.shape)],
        out_specs=pl.BlockSpec((tm, n), lambda i: (i, 0)),
        out_shape=jax.ShapeDtypeStruct((m, n), F32),
        compiler_params=_cp(("parallel",)),
        name=name,
    )(a, w)


def _attn_out_sample_kernel(a_ref, r_ref, w_ref, g_ref, b_ref, rows_in, slab_in, o_ref, slab_ref):
    del rows_in, slab_in
    o = _dot(a_ref[...].astype(BF16), w_ref[...])
    x2 = _ln(ALPHA * r_ref[...] + o, g_ref[...], b_ref[...])
    pad = o_ref.shape[0] - x2.shape[0]
    _store_rows_and_slab(jnp.concatenate([x2, jnp.zeros((pad, x2.shape[1]), F32)], axis=0), o_ref, slab_ref)


def _attn_out_sample(a, res, w, g, b, x2_rows, x2_slab, n_prompt):
    n = w.shape[1]
    m = x2_rows.shape[0] - n_prompt
    blk = n_prompt // m
    assert blk * m == n_prompt and m >= a.shape[0]
    any_spec = pl.BlockSpec(memory_space=pl.ANY)
    return pl.pallas_call(
        _attn_out_sample_kernel,
        grid=(1,),
        in_specs=[_full(a.shape), _full(res.shape), _full(w.shape), _full(g.shape), _full(b.shape),
                  any_spec, any_spec],
        out_specs=[pl.BlockSpec((m, n), lambda i: (blk, 0)),
                   pl.BlockSpec((m * ROW_SLAB, LANES), lambda i: (blk, 0))],
        out_shape=[jax.ShapeDtypeStruct(x2_rows.shape, F32), jax.ShapeDtypeStruct(x2_slab.shape, F32)],
        input_output_aliases={5: 0, 6: 1},
        compiler_params=_cp(("arbitrary",)),
        name="attn_out_sample",
    )(a, res, w, g, b, x2_rows, x2_slab)


def _attn_prompt_kernel(x_ref, k_ref, v_ref, wq_ref, wo_ref, g_ref, b_ref, o_ref, slab_ref):
    x = x_ref[0]
    q = _dot(x.astype(BF16), wq_ref[...])
    k = k_ref[0].astype(BF16)
    v = v_ref[0].astype(BF16)
    scale = XA_HEAD_DIM ** -0.5
    outs = []
    for h in range(XA_HEADS):
        sl = slice(h * XA_HEAD_DIM, (h + 1) * XA_HEAD_DIM)
        s = lax.dot_general(q[:, sl].astype(BF16), k[:, sl], (((1,), (1,)), ((), ())),
                            preferred_element_type=F32) * scale
        e = jnp.exp(s - jnp.max(s, axis=-1, keepdims=True))
        p = e / jnp.sum(e, axis=-1, keepdims=True)
        outs.append(_dot(p.astype(BF16), v[:, sl]))
    o = jnp.concatenate(outs, axis=-1).astype(BF16)
    att = _dot(o, wo_ref[...])
    _store_rows_and_slab(_ln(ALPHA * x + att, g_ref[...], b_ref[...]), o_ref, slab_ref)


def _store_rows_and_slab(x2, o_ref, slab_ref):
    rows = x2.shape[0]
    o_ref[...] = x2
    for c in range(ROW_SLAB):
        slab_ref[pl.ds(c, rows, stride=ROW_SLAB), :] = x2[:, c * LANES:(c + 1) * LANES]


def _attn_prompt(x1, mk, mv, wq, wo, g, b, t_all):
    nb, L, d = x1.shape
    nm = mk.shape[1]
    tm = ATT_TM
    per = L // tm
    return pl.pallas_call(
        _attn_prompt_kernel,
        grid=(nb, per),
        in_specs=[pl.BlockSpec((1, tm, d), lambda bi, i: (bi, i, 0)),
                  pl.BlockSpec((1, nm, d), lambda bi, i: (bi, 0, 0)),
                  pl.BlockSpec((1, nm, d), lambda bi, i: (bi, 0, 0)),
                  _full(wq.shape), _full(wo.shape), _full(g.shape), _full(b.shape)],
        out_specs=[pl.BlockSpec((tm, d), lambda bi, i: (bi * per + i, 0)),
                   pl.BlockSpec((tm * ROW_SLAB, LANES), lambda bi, i: (bi * per + i, 0))],
        out_shape=[jax.ShapeDtypeStruct((t_all, d), F32),
                   jax.ShapeDtypeStruct((t_all * ROW_SLAB, LANES), F32)],
        compiler_params=_cp(("parallel", "parallel")),
        name="attn_prompt",
    )(x1, mk, mv, wq, wo, g, b)


def _attn_sample_kernel(q_ref, k_ref, v_ref, o_ref):
    scale = XA_HEAD_DIM ** -0.5
    for gi in range(ATT_G):
        qn = q_ref[gi]
        s = jnp.sum(k_ref[gi] * qn[None], axis=-1, keepdims=True) * scale
        e = jnp.exp(s - jnp.max(s, axis=0, keepdims=True))
        p = e / jnp.sum(e, axis=0, keepdims=True)
        o_ref[gi] = jnp.sum(p * v_ref[gi], axis=0)


def _attn_sample(q, ck, cv):
    _, n, nm, nh, dh = ck.shape
    q3 = q.reshape(n, nh, dh)
    cache_spec = pl.BlockSpec((None, ATT_G, nm, nh, dh), lambda i: (0, i, 0, 0, 0))
    out = pl.pallas_call(
        _attn_sample_kernel,
        grid=(n // ATT_G,),
        in_specs=[pl.BlockSpec((ATT_G, nh, dh), lambda i: (i, 0, 0)), cache_spec, cache_spec],
        out_specs=pl.BlockSpec((ATT_G, nh, dh), lambda i: (i, 0, 0)),
        out_shape=jax.ShapeDtypeStruct((n, nh, dh), F32),
        compiler_params=_cp(("parallel",)),
        name="attn_sample",
    )(q3, ck, cv)
    return out.reshape(n, nh * dh)


def _first_argmax(val, iota, n, axis):
    mx = jnp.max(val, axis=axis, keepdims=True)
    idx = jnp.min(jnp.where(val == mx, iota, n), axis=axis, keepdims=True)
    return mx, idx


def _router_kernel(x_ref, wr_ref, rb_ref, idx_ref, gate_ref, rank_ref, ps_ref, be_ref, bn_ref, carry, *, t_real):
    tm = x_ref.shape[0]
    neg = -jnp.inf

    @pl.when(pl.program_id(1) == 0)
    def _():
        carry[...] = jnp.zeros(carry.shape, F32)

    logits = lax.dot_general(wr_ref[...], x_ref[...], (((1,), (1,)), ((), ())),
                             precision=lax.Precision.HIGHEST, preferred_element_type=F32)
    scores = _sigmoid(logits)
    biased = scores + rb_ref[...]
    grp = biased.reshape(N_GROUPS, GROUP_SIZE, tm)
    io_g = lax.broadcasted_iota(jnp.int32, (N_GROUPS, GROUP_SIZE, tm), 1)
    m1, i1 = _first_argmax(grp, io_g, GROUP_SIZE, 1)
    m2 = jnp.max(jnp.where(io_g == i1, neg, grp), axis=1, keepdims=True)
    gscore = (m1 + m2).reshape(N_GROUPS, tm)
    io_n = lax.broadcasted_iota(jnp.int32, (N_GROUPS, tm), 0)
    gsel = jnp.zeros((N_GROUPS, tm), jnp.bool_)
    for _ in range(TOPK_GROUPS):
        _, gi = _first_argmax(gscore, io_n, N_GROUPS, 0)
        hit = io_n == gi
        gsel = jnp.logical_or(gsel, hit)
        gscore = jnp.where(hit, neg, gscore)
    emask = jnp.broadcast_to(gsel.reshape(N_GROUPS, 1, tm), (N_GROUPS, GROUP_SIZE, tm)).reshape(N_EXPERTS, tm)
    val = jnp.where(emask, biased, neg)
    io_e = lax.broadcasted_iota(jnp.int32, (N_EXPERTS, tm), 0)
    idxs, ws, hits = [], [], []
    for _ in range(TOP_K):
        _, ei = _first_argmax(val, io_e, N_EXPERTS, 0)
        hit = io_e == ei
        ws.append(jnp.sum(jnp.where(hit, scores, 0.0), axis=0, keepdims=True))
        idxs.append(ei)
        hits.append(hit)
        val = jnp.where(hit, neg, val)
    w = jnp.concatenate(ws, axis=0)
    tile = pl.program_id(0) * pl.num_programs(1) + pl.program_id(1)
    valid = (tile * tm + lax.broadcasted_iota(jnp.int32, (1, tm), 1)) < t_real
    idx_all = jnp.where(valid, jnp.concatenate(idxs, axis=0), -1)
    gate_all = w / jnp.sum(w, axis=0, keepdims=True) * ROUTED_SCALE

    onehot = jnp.zeros((N_EXPERTS, tm), F32)
    for hit in hits:
        onehot = onehot + hit.astype(F32)
    onehot = jnp.where(valid, onehot, 0.0)
    tri = (lax.broadcasted_iota(jnp.int32, (tm, tm), 0) <= lax.broadcasted_iota(jnp.int32, (tm, tm), 1))
    csum = _dot(onehot.astype(BF16), tri.astype(BF16)) + carry[...]
    ranks = [jnp.sum(jnp.where(hit, csum, 0.0), axis=0, keepdims=True) - 1.0 for hit in hits]
    rank_all = jnp.concatenate(ranks, axis=0).astype(jnp.int32)
    for q in range(tm // LANES):
        rows = slice(q * TOP_K, (q + 1) * TOP_K)
        cols = slice(q * LANES, (q + 1) * LANES)
        idx_ref[rows, :] = idx_all[:, cols]
        gate_ref[rows, :] = gate_all[:, cols]
        rank_ref[rows, :] = rank_all[:, cols]
    counts = csum[:, tm - 1:tm]
    carry[...] = counts

    @pl.when(pl.program_id(1) == pl.num_programs(1) - 1)
    def _():
        r = float(MOE_R)
        padded = jnp.floor((counts + (r - 1.0)) / r) * r
        below = (lax.broadcasted_iota(jnp.int32, (N_EXPERTS, N_EXPERTS), 1)
                 < lax.broadcasted_iota(jnp.int32, (N_EXPERTS, N_EXPERTS), 0)).astype(F32)
        pstart = jnp.dot(below, jnp.broadcast_to(padded, (N_EXPERTS, LANES)), precision=lax.Precision.HIGHEST,
                         preferred_element_type=F32)[:, 0:1]
        pend = pstart + padded
        nb = be_ref.shape[-1]
        bpos = lax.broadcasted_iota(jnp.int32, (1, nb), 1).astype(F32) * r
        be = jnp.minimum(jnp.sum((bpos >= pend).astype(F32), axis=0, keepdims=True), float(N_EXPERTS - 1))
        sel = lax.broadcasted_iota(jnp.int32, (N_EXPERTS, nb), 0) == be.astype(jnp.int32)
        cnt_b = jnp.sum(jnp.where(sel, counts, 0.0), axis=0, keepdims=True)
        st_b = jnp.sum(jnp.where(sel, pstart, 0.0), axis=0, keepdims=True)
        ps_ref[0] = pstart.astype(jnp.int32)
        be_ref[0] = be.astype(jnp.int32)
        bn_ref[0] = jnp.clip(cnt_b - (bpos - st_b), 0.0, r).astype(jnp.int32)


def _router(x2, wr_t, rb, nseg, nblk_pad, t_real):
    t, d = x2.shape
    tm = TOK_TM
    tiles = t // (nseg * tm)
    tok_spec = pl.BlockSpec((tm // LANES * TOP_K, LANES), lambda s, j: (s * tiles + j, 0))
    tab_spec = pl.BlockSpec((1, 1, nblk_pad), lambda s, j: (s, 0, 0))
    tok_shape = (t // LANES * TOP_K, LANES)
    return pl.pallas_call(
        functools.partial(_router_kernel, t_real=t_real),
        grid=(nseg, tiles),
        in_specs=[pl.BlockSpec((tm, d), lambda s, j: (s * tiles + j, 0)), _full(wr_t.shape), _full(rb.shape)],
        out_specs=[tok_spec, tok_spec, tok_spec, pl.BlockSpec((1, N_EXPERTS, 1), lambda s, j: (s, 0, 0)),
                   tab_spec, tab_spec],
        out_shape=[jax.ShapeDtypeStruct(tok_shape, jnp.int32), jax.ShapeDtypeStruct(tok_shape, F32),
                   jax.ShapeDtypeStruct(tok_shape, jnp.int32),
                   jax.ShapeDtypeStruct((nseg, N_EXPERTS, 1), jnp.int32),
                   jax.ShapeDtypeStruct((nseg, 1, nblk_pad), jnp.int32),
                   jax.ShapeDtypeStruct((nseg, 1, nblk_pad), jnp.int32)],
        scratch_shapes=[pltpu.VMEM((N_EXPERTS, 1), F32)],
        compiler_params=_cp(("arbitrary", "arbitrary")),
        name="router",
    )(x2, wr_t, rb)


def _plan_sc(idx_f, rank_f, gate_f, pstart_f, nseg, tseg, nslots, lead):
    words = tseg * TOP_K
    chunk = words // 4
    assert chunk * 4 == words and chunk % 16 == 0 and nslots % 16 == 0
    info = plsc.get_sparse_core_info()
    ncores = info.num_cores
    nl = info.num_lanes
    mesh = plsc.VectorSubcoreMesh(core_axis_name="c", subcore_axis_name="s")
    dummy_off = tseg * ROW_SLAB

    def body(idx_hbm, rank_hbm, gate_hbm, ps_hbm, *rest):
        outs = rest[:3 * nseg]
        idx_v, rank_v, val_v, ps_v, slot_v = rest[3 * nseg:]
        wid = lax.axis_index("s") * ncores + lax.axis_index("c")
        lane = lax.iota(jnp.int32, nl)

        def run(seg, kind):
            want_gate = kind == 2
            out_hbm = outs[3 * seg + kind]
            pltpu.sync_copy(ps_hbm.at[pl.ds(seg * N_EXPERTS, N_EXPERTS)], ps_v)
            fill = jnp.full((nl,), dummy_off if kind == 1 else 0, jnp.int32)

            @pl.loop(0, nslots, step=nl)
            def _(i):
                slot_v[pl.ds(i, nl)] = fill

            @pl.loop(0, 4)
            def _(ci):
                base = seg * words + ci * chunk
                pltpu.sync_copy(idx_hbm.at[pl.ds(base, chunk)], idx_v)
                pltpu.sync_copy(rank_hbm.at[pl.ds(base, chunk)], rank_v)
                if want_gate:
                    pltpu.sync_copy(gate_hbm.at[pl.ds(base, chunk)], val_v)

                @pl.loop(0, chunk, step=nl)
                def _(o):
                    e = idx_v[pl.ds(o, nl)]
                    live = e >= 0
                    d = plsc.load_gather(ps_v, [jnp.maximum(e, 0)]) + rank_v[pl.ds(o, nl)] + lead
                    d = jnp.where(live, d, 0)
                    if want_gate:
                        plsc.store_scatter(slot_v, [d], val_v[pl.ds(o, nl)], mask=live)
                    else:
                        f = ci * chunk + o + lane
                        tok = ((f >> 10) << 7) | (f & (LANES - 1))
                        plsc.store_scatter(slot_v, [d], tok * ROW_SLAB, mask=live)

            pltpu.sync_copy(slot_v, out_hbm)

        for seg in range(nseg):
            for kind in range(3):
                @pl.when(wid == 3 * seg + kind)
                def _(seg=seg, kind=kind):
                    run(seg, kind)

    out_type = [jax.ShapeDtypeStruct((nslots,), jnp.int32) for _ in range(3 * nseg)]
    fn = pl.kernel(
        body, out_type=out_type, mesh=mesh,
        scratch_types=[pltpu.VMEM((chunk,), jnp.int32), pltpu.VMEM((chunk,), jnp.int32),
                       pltpu.VMEM((chunk,), jnp.int32), pltpu.VMEM((N_EXPERTS,), jnp.int32),
                       pltpu.VMEM((nslots,), jnp.int32)],
        compiler_params=pltpu.CompilerParams(needs_layout_passes=False), name="plan_sc")
    return fn(idx_f, rank_f, lax.bitcast_convert_type(gate_f, jnp.int32), pstart_f)


def _expert_kernel(be_ref, bn_ref, slotg_ref, slots_ref, x_ref, ge_ref, go_ref,
                   w1e_ref, w3e_ref, w2e_ref, w1o_ref, w3o_ref, w2o_ref, *rest):
    acc_ref, tile_a, tile_b, ys_a, ys_b = rest[-5:]
    _expert_body(bn_ref, slotg_ref, slots_ref, x_ref, (ge_ref, w1e_ref, w3e_ref, w2e_ref),
                 (go_ref, w1o_ref, w3o_ref, w2o_ref), acc_ref, tile_a, tile_b, ys_a, ys_b)
    del be_ref


def _expert_body(bn_ref, slotg_ref, slots_ref, x_ref, even_refs, odd_refs,
                 acc_ref, tile_a, tile_b, ys_a, ys_b):
    i = pl.program_id(0)
    r = MOE_R
    s = TILE_S
    zero_rows = 64 * ROW_SLAB

    group = 4

    def gather_moves(blk, tile):
        base = (blk + 1) * r

        def move(q):
            for row in range(q * group, (q + 1) * group):
                off = pl.multiple_of(slotg_ref[base + row], ROW_SLAB)
                tile[pl.ds(row, ROW_SLAB, stride=s), :] = x_ref[pl.ds(off, ROW_SLAB), :]

        return [functools.partial(move, q) for q in range(r // group)]

    def scatter_moves(blk, ys):
        base = (blk + 1) * r

        def move(q):
            offs, vals = [], []
            for row in range(q * group, (q + 1) * group):
                off = pl.multiple_of(slots_ref[base + row], ROW_SLAB)
                src = (row // SUBLANES) * (ROW_SLAB * SUBLANES) + row % SUBLANES
                vals.append(acc_ref[pl.ds(off, ROW_SLAB), :] + ys[pl.ds(src, ROW_SLAB, stride=SUBLANES), :])
                offs.append(off)
            for off, val in zip(offs, vals):
                acc_ref[pl.ds(off, ROW_SLAB), :] = val

        return [functools.partial(move, q) for q in range(r // group)]

    def ffn_stages(tile, refs, ys):
        g_ref, w1_ref, w3_ref, w2_ref = refs
        xb = jnp.concatenate([tile[c * s:c * s + r, :] for c in range(ROW_SLAB)], axis=-1).astype(BF16)
        h1 = _dot(xb, w1_ref[0])
        yield
        h3 = _dot(xb, w3_ref[0])
        yield
        gcol = jnp.transpose(jnp.broadcast_to(g_ref[0], (SUBLANES, r)))[:, 0:1]
        hh = (_silu(h1) * h3).astype(BF16)
        wide = 2 * LANES
        for nc in range(D_MODEL // wide):
            y = _dot(hh, w2_ref[0, :, nc * wide:(nc + 1) * wide]) * gcol
            for g in range(r // SUBLANES):
                for cc in range(wide // LANES):
                    c = nc * (wide // LANES) + cc
                    ys[pl.ds((g * ROW_SLAB + c) * SUBLANES, SUBLANES), :] = (
                        y[g * SUBLANES:(g + 1) * SUBLANES, cc * LANES:(cc + 1) * LANES])
            yield

    def run_phase(stages, moves):
        n_stage = 2 + D_MODEL // (2 * LANES)
        per = -(-len(moves) // n_stage)
        for k, _ in enumerate(stages):
            for mv in moves[k * per:(k + 1) * per]:
                mv()

    def run_all(moves):
        for mv in moves:
            mv()

    @pl.when(i == 0)
    def _():
        def zero(k, c):
            acc_ref[pl.ds(pl.multiple_of(k * zero_rows, zero_rows), zero_rows), :] = jnp.zeros((zero_rows, LANES), F32)
            return c

        lax.fori_loop(0, acc_ref.shape[0] // zero_rows, zero, 0)
        rem = acc_ref.shape[0] % zero_rows
        if rem:
            acc_ref[pl.ds(acc_ref.shape[0] - rem, rem), :] = jnp.zeros((rem, LANES), F32)
        tile_a[...] = jnp.zeros(tile_a.shape, F32)
        tile_b[...] = jnp.zeros(tile_b.shape, F32)
        ys_b[...] = jnp.zeros(ys_b.shape, F32)
        run_all(gather_moves(0, tile_a))

    even = 2 * i
    odd = even + 1
    pending = bn_ref[even] + bn_ref[odd] + bn_ref[jnp.maximum(even - 1, 0)]

    @pl.when(pending > 0)
    def _():
        def mixed(a, b):
            return [mv for pair in zip(a, b) for mv in pair]

        run_phase(ffn_stages(tile_a, even_refs, ys_a),
                  mixed(gather_moves(odd, tile_b), scatter_moves(even - 1, ys_b)))
        run_phase(ffn_stages(tile_b, odd_refs, ys_b),
                  mixed(gather_moves(even + 2, tile_a), scatter_moves(even, ys_a)))


def _expert_segment(si, nseg, blk_e, blk_n, slots_g, slots_s, gates3, xslab3, w1, w3, w2, tseg, nblk,
                    routed_prev):
    r = MOE_R
    acc_rows = (tseg + SUBLANES) * ROW_SLAB
    one = pl.Buffered(1)
    w1_blk = (1, D_MODEL, D_EXPERT)
    w2_blk = (1, D_EXPERT, D_MODEL)
    g_blk = (1, 1, r)
    in_specs = [pl.BlockSpec((None, tseg * ROW_SLAB, LANES), lambda i, *_: (si, 0, 0), pipeline_mode=one),
                pl.BlockSpec(g_blk, lambda i, *_: (2 * i + 1, 0, 0)),
                pl.BlockSpec(g_blk, lambda i, *_: (2 * i + 2, 0, 0)),
                pl.BlockSpec(w1_blk, lambda i, be, *_: (be[2 * i], 0, 0)),
                pl.BlockSpec(w1_blk, lambda i, be, *_: (be[2 * i], 0, 0)),
                pl.BlockSpec(w2_blk, lambda i, be, *_: (be[2 * i], 0, 0)),
                pl.BlockSpec(w1_blk, lambda i, be, *_: (be[2 * i + 1], 0, 0)),
                pl.BlockSpec(w1_blk, lambda i, be, *_: (be[2 * i + 1], 0, 0)),
                pl.BlockSpec(w2_blk, lambda i, be, *_: (be[2 * i + 1], 0, 0))]
    args = [blk_e, blk_n, slots_g, slots_s, xslab3, gates3, gates3, w1, w3, w2, w1, w3, w2]
    aliases = {}
    if routed_prev is not None:
        in_specs.append(pl.BlockSpec(memory_space=pl.ANY))
        aliases = {len(args): 0}
        args.append(routed_prev)
    grid_spec = pltpu.PrefetchScalarGridSpec(
        num_scalar_prefetch=4,
        grid=(nblk // 2 + 1,),
        in_specs=in_specs,
        out_specs=pl.BlockSpec((None, acc_rows, LANES), lambda i, *_: (si, 0, 0), pipeline_mode=one),
        scratch_shapes=[pltpu.VMEM((ROW_SLAB * TILE_S, LANES), F32),
                        pltpu.VMEM((ROW_SLAB * TILE_S, LANES), F32),
                        pltpu.VMEM((r * ROW_SLAB, LANES), F32),
                        pltpu.VMEM((r * ROW_SLAB, LANES), F32)],
    )
    return pl.pallas_call(
        _expert_kernel,
        grid_spec=grid_spec,
        out_shape=jax.ShapeDtypeStruct((nseg, acc_rows, LANES), F32),
        input_output_aliases=aliases,
        compiler_params=_cp(("arbitrary",)),
        name="experts",
    )(*args)


def _final_kernel(x_ref, r_ref, w13_ref, w2_ref, g_ref, b_ref, yp_ref, ys_ref, *, tiles, p_tiles):
    tm = x_ref.shape[0]
    x = x_ref[...]
    h13 = _dot(x.astype(BF16), w13_ref[...])
    hh = (_silu(h13[:, :D_EXPERT]) * h13[:, D_EXPERT:]).astype(BF16)
    sh = _dot(hh, w2_ref[...])
    routed = jnp.concatenate([r_ref[pl.ds(c, tm, stride=ROW_SLAB), :] for c in range(ROW_SLAB)], axis=-1)
    y = _ln(ALPHA * x + (sh + routed), g_ref[...], b_ref[...])
    tile = pl.program_id(0) * tiles + pl.program_id(1)

    @pl.when(tile < p_tiles)
    def _():
        yp_ref[...] = y

    @pl.when(tile >= p_tiles)
    def _():
        ys_ref[...] = y[:ys_ref.shape[0], :]


def _final(x2, routed3, w13, w2, g, b, n_prompt, n_sample):
    t, d = x2.shape
    nseg = routed3.shape[0]
    tm = TOK_TM
    tiles = t // (nseg * tm)
    p_tiles = n_prompt // tm
    assert p_tiles * tm == n_prompt and t - n_prompt == tm and n_sample <= tm
    return pl.pallas_call(
        functools.partial(_final_kernel, tiles=tiles, p_tiles=p_tiles),
        grid=(nseg, tiles),
        in_specs=[pl.BlockSpec((tm, d), lambda s, j: (s * tiles + j, 0)),
                  pl.BlockSpec((None, tm * ROW_SLAB, LANES), lambda s, j: (s, j, 0)),
                  _full(w13.shape), _full(w2.shape), _full(g.shape), _full(b.shape)],
        out_specs=[pl.BlockSpec((tm, d), lambda s, j: (jnp.minimum(s * tiles + j, p_tiles - 1), 0)),
                   pl.BlockSpec((n_sample, d), lambda s, j: (0, 0))],
        out_shape=[jax.ShapeDtypeStruct((n_prompt, d), F32), jax.ShapeDtypeStruct((n_sample, d), F32)],
        compiler_params=_cp(("arbitrary", "arbitrary")),
        name="final",
    )(x2, routed3, w13, w2, g, b)


def _block_diag_dense(w):
    h, dh, _ = w.shape
    eye = jnp.eye(h, dtype=w.dtype)
    return jnp.einsum('hij,hg->higj', w, eye).reshape(h * dh, h * dh)


def kernel(x_prompt, x_sample, mem_prompt, state_rglru_h, state_rglru_conv, state_dwconv, cache_mem_k, cache_mem_v, w_in, b_in, w_conv_a, b_conv_a, w_rg_a, b_rg_a, w_rg_x, b_rg_x, rg_lambda, w_conv_b, b_conv_b, ln_c_g, ln_c_b, w_out, ln1_g, ln1_b, w_q, w_k, w_v, w_o, ln2_g, ln2_b, w_router, router_bias, w1_e, w3_e, w2_e, w1_s, w3_s, w2_s, ln3_g, ln3_b):
    assert w_in.shape[0] == 1, "single-layer trunk"
    nb, L, d = x_prompt.shape
    ns = x_sample.shape[0]
    nm = mem_prompt.shape[1]
    row = lambda a: a.reshape(1, -1)

    wp = dict(
        w_in=w_in[0].astype(BF16), b_in=row(b_in[0]),
        w_conv_a=w_conv_a[0], b_conv_a=row(b_conv_a[0]),
        w_gate=jnp.concatenate([_block_diag_dense(w_rg_a[0]), _block_diag_dense(w_rg_x[0])], axis=1).astype(BF16),
        b_gate=row(jnp.concatenate([b_rg_a[0], b_rg_x[0]])),
        lam=row(rg_lambda[0]),
        w_conv_b=w_conv_b[0], b_conv_b=row(b_conv_b[0]),
        ln_c_g=row(ln_c_g[0]), ln_c_b=row(ln_c_b[0]),
        w_out=w_out[0].astype(BF16), ln1_g=row(ln1_g[0]), ln1_b=row(ln1_b[0]),
    )
    wq = w_q[0].astype(BF16)
    wo = w_o[0].astype(BF16)
    wkv = jnp.concatenate([w_k[0], w_v[0]], axis=1).astype(BF16)
    l2g, l2b = row(ln2_g[0]), row(ln2_b[0])

    x1_p, h_p, ca_p, cb_p = _mix_prompt(x_prompt, wp)
    kv = _linear(mem_prompt.reshape(nb * nm, d), wkv, 512, "kv_proj")
    mk_p = kv[:, :d].reshape(nb, nm, d)
    mv_p = kv[:, d:].reshape(nb, nm, d)
    t_real = nb * L + ns
    t = -(-t_real // (MOE_NSEG * TOK_TM)) * (MOE_NSEG * TOK_TM)
    x2, x2_slab = _attn_prompt(x1_p, mk_p, mv_p, wq, wo, l2g, l2b, t)

    xs = x_sample.reshape(ns, d)
    sta = jnp.transpose(state_rglru_conv[0], (1, 0, 2))
    stb = jnp.transpose(state_dwconv[0], (1, 0, 2))
    x1_s, h_s, u_s, glu_s = _mix_sample(xs, sta, stb, state_rglru_h[0], wp)
    q_s = _linear(x1_s, wq, ns, "q_sample")
    o_s = _attn_sample(q_s, cache_mem_k, cache_mem_v)
    x2, x2_slab = _attn_out_sample(o_s, x1_s, wo, l2g, l2b, x2, x2_slab, nb * L)

    nseg = MOE_NSEG
    tseg = t // nseg
    assert tseg * nseg == t and tseg % LANES == 0 and t % TOK_TM == 0
    nblk = -(-(tseg * TOP_K + N_EXPERTS * (MOE_R - 1)) // MOE_R)
    nblk += nblk % 2
    nblk_pad = -(-(nblk + 2) // LANES) * LANES
    idx_f, gate_f, rank_f, pstart, blk_e, blk_n = _router(
        x2, jnp.transpose(w_router[0]), router_bias[0].reshape(N_EXPERTS, 1), nseg, nblk_pad, t_real)
    plan = _plan_sc(idx_f.reshape(-1), rank_f.reshape(-1), gate_f.reshape(-1), pstart.reshape(-1),
                    nseg, tseg, (nblk + 4) * MOE_R, MOE_R)
    w1_eb, w3_eb, w2_eb = w1_e[0].astype(BF16), w3_e[0].astype(BF16), w2_e[0].astype(BF16)
    xslab3 = x2_slab.reshape(nseg, tseg * ROW_SLAB, LANES)
    routed3 = None
    for si in range(nseg):
        gates3 = lax.bitcast_convert_type(plan[3 * si + 2], F32).reshape(nblk + 4, 1, MOE_R)
        routed3 = _expert_segment(si, nseg, blk_e[si, 0], blk_n[si, 0], plan[3 * si], plan[3 * si + 1], gates3,
                                  xslab3, w1_eb, w3_eb, w2_eb, tseg, nblk, routed3)
    w13_s = jnp.concatenate([w1_s[0], w3_s[0]], axis=-1).astype(BF16)
    y_p, y_s = _final(x2, routed3, w13_s, w2_s[0].astype(BF16), row(ln3_g[0]), row(ln3_b[0]), nb * L, ns)
    y_p = y_p.reshape(nb, L, d)
    y_s = y_s.reshape(ns, 1, d)
    new_ca_s = jnp.concatenate([state_rglru_conv[0][:, 1:], u_s[:, None, :]], axis=1)
    new_cb_s = jnp.concatenate([state_dwconv[0][:, 1:], glu_s[:, None, :]], axis=1)
    hd = XA_HEAD_DIM
    return (y_p, y_s, h_p[None], ca_p[None], cb_p[None],
            mk_p.reshape(1, nb, nm, XA_HEADS, hd), mv_p.reshape(1, nb, nm, XA_HEADS, hd),
            h_s[None], new_ca_s[None], new_cb_s[None])
```

```python
import functools

import jax
import jax.numpy as jnp
from jax import lax
from jax.experimental import pallas as pl
from jax.experimental.pallas import tpu as pltpu
from jax.experimental.pallas import tpu_sc as plsc

F32 = jnp.float32
BF16 = jnp.bfloat16

D_MODEL = 1024
D_REC = 512
D_CONV = 512
REC_HEADS = 8
REC_HEAD_DIM = D_REC // REC_HEADS
REC_CONV_W = 4
RG_C = 8.0
CONF_CONV_W = 31
XA_HEADS = 4
XA_HEAD_DIM = D_MODEL // XA_HEADS
N_EXPERTS = 64
TOP_K = 8
N_GROUPS = 8
TOPK_GROUPS = 4
GROUP_SIZE = N_EXPERTS // N_GROUPS
D_EXPERT = D_MODEL // 4
ROUTED_SCALE = 2.5
LN_EPS = 1e-5
DEPTH = 1
ALPHA = (2 * DEPTH) ** 0.25

LANES = 128
SUBLANES = 8
VMEM_LIMIT = 58 * 1024 * 1024

MIX_TL = 128
ATT_TM = 1024
ATT_G = 4
TOK_TM = 512
MOE_NSEG = 3
MOE_R = 256
ROW_SLAB = D_MODEL // LANES
TILE_S = MOE_R + 8


def _cp(sem=None):
    return pltpu.CompilerParams(dimension_semantics=sem, vmem_limit_bytes=VMEM_LIMIT)


def _ln(x, g, b):
    mu = jnp.mean(x, axis=-1, keepdims=True)
    xc = x - mu
    var = jnp.mean(xc * xc, axis=-1, keepdims=True)
    return xc * lax.rsqrt(var + LN_EPS) * g + b


def _sigmoid(x):
    return 1.0 / (1.0 + jnp.exp(-x))


def _silu(x):
    return x * _sigmoid(x)


def _softplus(x):
    return jnp.maximum(x, 0.0) + jnp.log1p(jnp.exp(-jnp.abs(x)))


def _gelu_tanh(x):
    c = 0.7978845608028654
    return 0.5 * x * (1.0 + jnp.tanh(c * (x + 0.044715 * (x * x * x))))


def _dot(a, b):
    return jnp.dot(a, b, preferred_element_type=F32)


def _full(shape):
    n = len(shape)
    return pl.BlockSpec(shape, lambda *_: (0,) * n)


def _rglru_coeffs(ucf, wg_ref, bg_ref, lam_ref):
    ub = ucf.astype(BF16)
    half = D_REC // 2
    pre = [jnp.concatenate([_dot(ub[:, p * half:(p + 1) * half], wg_ref[g, p]) for p in range(2)], axis=-1)
           for g in range(2)]
    r = _sigmoid(pre[0] + bg_ref[:, :D_REC])
    ig = _sigmoid(pre[1] + bg_ref[:, D_REC:])
    log_a = (-RG_C) * r * _softplus(-lam_ref[...])
    a = jnp.exp(log_a)
    bt = jnp.sqrt(-jnp.tanh(log_a) * (a * a + 1.0)) * (ig * ucf)
    return a, bt


def _mix_prompt_kernel(x_ref, win_ref, bin_ref, wca_ref, bca_ref, wg_ref, bg_ref, lam_ref,
                       wcb_ref, bcb_ref, lcg_ref, lcb_ref, wout_ref, l1g_ref, l1b_ref,
                       x1_ref, hl_ref, ca_ref, cb_ref,
                       xs, exta, extb, hc, *, nb, tl):
    i = pl.program_id(0)
    m = nb * tl
    wa = REC_CONV_W
    wb = CONF_CONV_W
    ha = (wa - 1) * nb
    hb = (wb - 1) * nb
    nch = D_MODEL // LANES

    @pl.when(i == 0)
    def _():
        exta[0:ha, :] = jnp.zeros((ha, D_REC), F32)
        extb[0:hb, :] = jnp.zeros((hb, D_CONV), F32)
        hc[...] = jnp.zeros((nb, D_REC), F32)

    for n in range(nb):
        for c in range(nch):
            xs[c, pl.ds(n, tl, stride=nb), :] = x_ref[n, :, c * LANES:(c + 1) * LANES]
    xf = jnp.concatenate([xs[c] for c in range(nch)], axis=-1)
    z = _dot(xf.astype(BF16), win_ref[...]) + bin_ref[...]
    g = z[:, 0:D_REC]
    u = z[:, D_REC:2 * D_REC]
    v = z[:, 2 * D_REC:2 * D_REC + D_CONV]
    gg = z[:, 2 * D_REC + D_CONV:]

    exta[ha:ha + m, :] = u
    uc = jnp.zeros((m, D_REC), F32) + bca_ref[...]
    for j in range(wa):
        uc = uc + wca_ref[j:j + 1, :] * exta[j * nb:j * nb + m, :]
    tail_a = exta[m:m + ha, :]
    ca_ref[...] = tail_a.reshape(wa - 1, nb, D_REC)
    exta[0:ha, :] = tail_a

    a, bt = _rglru_coeffs(uc, wg_ref, bg_ref, lam_ref)

    h = hc[...]
    hs = []
    for t in range(tl):
        h = a[t * nb:(t + 1) * nb, :] * h + bt[t * nb:(t + 1) * nb, :]
        hs.append(h)
    hc[...] = h
    hl_ref[...] = h
    y_a = _gelu_tanh(g) * jnp.concatenate(hs, axis=0)

    extb[hb:hb + m, :] = v * _sigmoid(gg)
    c = jnp.zeros((m, D_CONV), F32) + bcb_ref[...]
    for j in range(wb):
        c = c + wcb_ref[j:j + 1, :] * extb[j * nb:j * nb + m, :]
    tail_b = extb[m:m + hb, :]
    cb_ref[...] = tail_b.reshape(wb - 1, nb, D_CONV)
    extb[0:hb, :] = tail_b
    y_b = _silu(_ln(c, lcg_ref[...], lcb_ref[...]))

    mix = jnp.concatenate([y_a, y_b], axis=-1).astype(BF16)
    o = _dot(mix, wout_ref[...])
    x1 = _ln(ALPHA * xf + o, l1g_ref[...], l1b_ref[...])
    for c in range(nch):
        xs[c] = x1[:, c * LANES:(c + 1) * LANES]
    for n in range(nb):
        for c in range(nch):
            x1_ref[n, :, c * LANES:(c + 1) * LANES] = xs[c, pl.ds(n, tl, stride=nb), :]


def _mix_prompt(x, wp):
    nb, L, _ = x.shape
    tl = MIX_TL
    m = nb * tl
    kern = functools.partial(_mix_prompt_kernel, nb=nb, tl=tl)
    ins = [wp['w_in'], wp['b_in'], wp['w_conv_a'], wp['b_conv_a'], wp['w_gate'], wp['b_gate'],
           wp['lam'], wp['w_conv_b'], wp['b_conv_b'], wp['ln_c_g'], wp['ln_c_b'], wp['w_out'],
           wp['ln1_g'], wp['ln1_b']]
    x1, h, ca, cb = pl.pallas_call(
        kern,
        grid=(L // tl,),
        in_specs=[pl.BlockSpec((nb, tl, D_MODEL), lambda i: (0, i, 0))] + [_full(a.shape) for a in ins],
        out_specs=[pl.BlockSpec((nb, tl, D_MODEL), lambda i: (0, i, 0)),
                   _full((nb, D_REC)), _full((REC_CONV_W - 1, nb, D_REC)),
                   _full((CONF_CONV_W - 1, nb, D_CONV))],
        out_shape=[jax.ShapeDtypeStruct((nb, L, D_MODEL), F32),
                   jax.ShapeDtypeStruct((nb, D_REC), F32),
                   jax.ShapeDtypeStruct((REC_CONV_W - 1, nb, D_REC), F32),
                   jax.ShapeDtypeStruct((CONF_CONV_W - 1, nb, D_CONV), F32)],
        scratch_shapes=[pltpu.VMEM((D_MODEL // LANES, m, LANES), F32),
                        pltpu.VMEM(((REC_CONV_W - 1) * nb + m, D_REC), F32),
                        pltpu.VMEM(((CONF_CONV_W - 1) * nb + m, D_CONV), F32),
                        pltpu.VMEM((nb, D_REC), F32)],
        compiler_params=_cp(("arbitrary",)),
        name="mix_prompt",
    )(x, *ins)
    return x1, h, jnp.transpose(ca, (1, 0, 2)), jnp.transpose(cb, (1, 0, 2))


def _mix_sample_kernel(x_ref, sta_ref, stb_ref, h0_ref, win_ref, bin_ref, wca_ref, bca_ref, wg_ref,
                       bg_ref, lam_ref, wcb_ref, bcb_ref, lcg_ref, lcb_ref, wout_ref, l1g_ref, l1b_ref,
                       x1_ref, h_ref, u_ref, glu_ref):
    wa = REC_CONV_W
    wb = CONF_CONV_W
    xf = x_ref[...]
    z = _dot(xf.astype(BF16), win_ref[...]) + bin_ref[...]
    g = z[:, 0:D_REC]
    u = z[:, D_REC:2 * D_REC]
    v = z[:, 2 * D_REC:2 * D_REC + D_CONV]
    gg = z[:, 2 * D_REC + D_CONV:]

    uc = bca_ref[...] + wca_ref[wa - 1:wa, :] * u
    for j in range(wa - 1):
        uc = uc + wca_ref[j:j + 1, :] * sta_ref[j]
    a, bt = _rglru_coeffs(uc, wg_ref, bg_ref, lam_ref)
    h = a * h0_ref[...] + bt
    y_a = _gelu_tanh(g) * h

    glu = v * _sigmoid(gg)
    c = bcb_ref[...] + wcb_ref[wb - 1:wb, :] * glu
    for j in range(wb - 1):
        c = c + wcb_ref[j:j + 1, :] * stb_ref[j]
    y_b = _silu(_ln(c, lcg_ref[...], lcb_ref[...]))

    mix = jnp.concatenate([y_a, y_b], axis=-1).astype(BF16)
    o = _dot(mix, wout_ref[...])
    x1_ref[...] = _ln(ALPHA * xf + o, l1g_ref[...], l1b_ref[...])
    h_ref[...] = h
    u_ref[...] = u
    glu_ref[...] = glu


def _mix_sample(x, sta, stb, h0, wp):
    n = x.shape[0]
    ins = [x, sta, stb, h0, wp['w_in'], wp['b_in'], wp['w_conv_a'], wp['b_conv_a'], wp['w_gate'],
           wp['b_gate'], wp['lam'], wp['w_conv_b'], wp['b_conv_b'], wp['ln_c_g'], wp['ln_c_b'],
           wp['w_out'], wp['ln1_g'], wp['ln1_b']]
    return pl.pallas_call(
        _mix_sample_kernel,
        grid=(1,),
        in_specs=[_full(a.shape) for a in ins],
        out_specs=[_full((n, D_MODEL)), _full((n, D_REC)), _full((n, D_REC)), _full((n, D_CONV))],
        out_shape=[jax.ShapeDtypeStruct((n, D_MODEL), F32), jax.ShapeDtypeStruct((n, D_REC), F32),
                   jax.ShapeDtypeStruct((n, D_REC), F32), jax.ShapeDtypeStruct((n, D_CONV), F32)],
        compiler_params=_cp(("arbitrary",)),
        name="mix_sample",
    )(*ins)


def _linear_kernel(a_ref, w_ref, o_ref):
    o_ref[...] = _dot(a_ref[...].astype(BF16), w_ref[...])


def _linear(a, w, tm, name):
    m, k = a.shape
    n = w.shape[1]
    return pl.pallas_call(
        _linear_kernel,
        grid=(m // tm,),
        in_specs=[pl.BlockSpec((tm, k), lambda i: (i, 0)), _full(w.shape)],
        out_specs=pl.BlockSpec((tm, n), lambda i: (i, 0)),
        out_shape=jax.ShapeDtypeStruct((m, n), F32),
        compiler_params=_cp(("parallel",)),
        name=name,
    )(a, w)


def _attn_out_sample_kernel(a_ref, r_ref, w_ref, g_ref, b_ref, rows_in, slab_in, o_ref, slab_ref):
    del rows_in, slab_in
    o = _dot(a_ref[...].astype(BF16), w_ref[...])
    x2 = _ln(ALPHA * r_ref[...] + o, g_ref[...], b_ref[...])
    pad = o_ref.shape[0] - x2.shape[0]
    _store_rows_and_slab(jnp.concatenate([x2, jnp.zeros((pad, x2.shape[1]), F32)], axis=0), o_ref, slab_ref)


def _attn_out_sample(a, res, w, g, b, x2_rows, x2_slab, n_prompt):
    n = w.shape[1]
    m = x2_rows.shape[0] - n_prompt
    blk = n_prompt // m
    assert blk * m == n_prompt and m >= a.shape[0]
    any_spec = pl.BlockSpec(memory_space=pl.ANY)
    return pl.pallas_call(
        _attn_out_sample_kernel,
        grid=(1,),
        in_specs=[_full(a.shape), _full(res.shape), _full(w.shape), _full(g.shape), _full(b.shape),
                  any_spec, any_spec],
        out_specs=[pl.BlockSpec((m, n), lambda i: (blk, 0)),
                   pl.BlockSpec((m * ROW_SLAB, LANES), lambda i: (blk, 0))],
        out_shape=[jax.ShapeDtypeStruct(x2_rows.shape, F32), jax.ShapeDtypeStruct(x2_slab.shape, F32)],
        input_output_aliases={5: 0, 6: 1},
        compiler_params=_cp(("arbitrary",)),
        name="attn_out_sample",
    )(a, res, w, g, b, x2_rows, x2_slab)


def _attn_prompt_kernel(x_ref, k_ref, v_ref, wq_ref, wo_ref, g_ref, b_ref, o_ref, slab_ref):
    x = x_ref[0]
    q = _dot(x.astype(BF16), wq_ref[...])
    k = k_ref[0].astype(BF16)
    v = v_ref[0].astype(BF16)
    scale = XA_HEAD_DIM ** -0.5
    outs = []
    for h in range(XA_HEADS):
        sl = slice(h * XA_HEAD_DIM, (h + 1) * XA_HEAD_DIM)
        s = lax.dot_general(q[:, sl].astype(BF16), k[:, sl], (((1,), (1,)), ((), ())),
                            preferred_element_type=F32) * scale
        e = jnp.exp(s - jnp.max(s, axis=-1, keepdims=True))
        p = e / jnp.sum(e, axis=-1, keepdims=True)
        outs.append(_dot(p.astype(BF16), v[:, sl]))
    o = jnp.concatenate(outs, axis=-1).astype(BF16)
    att = _dot(o, wo_ref[...])
    _store_rows_and_slab(_ln(ALPHA * x + att, g_ref[...], b_ref[...]), o_ref, slab_ref)


def _store_rows_and_slab(x2, o_ref, slab_ref):
    rows = x2.shape[0]
    o_ref[...] = x2
    for c in range(ROW_SLAB):
        slab_ref[pl.ds(c, rows, stride=ROW_SLAB), :] = x2[:, c * LANES:(c + 1) * LANES]


def _attn_prompt(x1, mk, mv, wq, wo, g, b, t_all):
    nb, L, d = x1.shape
    nm = mk.shape[1]
    tm = ATT_TM
    per = L // tm
    return pl.pallas_call(
        _attn_prompt_kernel,
        grid=(nb, per),
        in_specs=[pl.BlockSpec((1, tm, d), lambda bi, i: (bi, i, 0)),
                  pl.BlockSpec((1, nm, d), lambda bi, i: (bi, 0, 0)),
                  pl.BlockSpec((1, nm, d), lambda bi, i: (bi, 0, 0)),
                  _full(wq.shape), _full(wo.shape), _full(g.shape), _full(b.shape)],
        out_specs=[pl.BlockSpec((tm, d), lambda bi, i: (bi * per + i, 0)),
                   pl.BlockSpec((tm * ROW_SLAB, LANES), lambda bi, i: (bi * per + i, 0))],
        out_shape=[jax.ShapeDtypeStruct((t_all, d), F32),
                   jax.ShapeDtypeStruct((t_all * ROW_SLAB, LANES), F32)],
        compiler_params=_cp(("parallel", "parallel")),
        name="attn_prompt",
    )(x1, mk, mv, wq, wo, g, b)


def _attn_sample_kernel(q_ref, k_ref, v_ref, o_ref):
    scale = XA_HEAD_DIM ** -0.5
    for gi in range(ATT_G):
        qn = q_ref[gi]
        s = jnp.sum(k_ref[gi] * qn[None], axis=-1, keepdims=True) * scale
        e = jnp.exp(s - jnp.max(s, axis=0, keepdims=True))
        p = e / jnp.sum(e, axis=0, keepdims=True)
        o_ref[gi] = jnp.sum(p * v_ref[gi], axis=0)


def _attn_sample(q, ck, cv):
    _, n, nm, nh, dh = ck.shape
    q3 = q.reshape(n, nh, dh)
    cache_spec = pl.BlockSpec((None, ATT_G, nm, nh, dh), lambda i: (0, i, 0, 0, 0))
    out = pl.pallas_call(
        _attn_sample_kernel,
        grid=(n // ATT_G,),
        in_specs=[pl.BlockSpec((ATT_G, nh, dh), lambda i: (i, 0, 0)), cache_spec, cache_spec],
        out_specs=pl.BlockSpec((ATT_G, nh, dh), lambda i: (i, 0, 0)),
        out_shape=jax.ShapeDtypeStruct((n, nh, dh), F32),
        compiler_params=_cp(("parallel",)),
        name="attn_sample",
    )(q3, ck, cv)
    return out.reshape(n, nh * dh)


def _first_argmax(val, iota, n, axis):
    mx = jnp.max(val, axis=axis, keepdims=True)
    idx = jnp.min(jnp.where(val == mx, iota, n), axis=axis, keepdims=True)
    return mx, idx


def _router_kernel(x_ref, wr_ref, rb_ref, idx_ref, gate_ref, rank_ref, ps_ref, be_ref, bn_ref, carry, *, t_real):
    tm = x_ref.shape[0]
    neg = -jnp.inf

    @pl.when(pl.program_id(1) == 0)
    def _():
        carry[...] = jnp.zeros(carry.shape, F32)

    logits = lax.dot_general(wr_ref[...], x_ref[...], (((1,), (1,)), ((), ())),
                             precision=lax.Precision.HIGHEST, preferred_element_type=F32)
    scores = _sigmoid(logits)
    biased = scores + rb_ref[...]
    grp = biased.reshape(N_GROUPS, GROUP_SIZE, tm)
    io_g = lax.broadcasted_iota(jnp.int32, (N_GROUPS, GROUP_SIZE, tm), 1)
    m1, i1 = _first_argmax(grp, io_g, GROUP_SIZE, 1)
    m2 = jnp.max(jnp.where(io_g == i1, neg, grp), axis=1, keepdims=True)
    gscore = (m1 + m2).reshape(N_GROUPS, tm)
    io_n = lax.broadcasted_iota(jnp.int32, (N_GROUPS, tm), 0)
    gsel = jnp.zeros((N_GROUPS, tm), jnp.bool_)
    for _ in range(TOPK_GROUPS):
        _, gi = _first_argmax(gscore, io_n, N_GROUPS, 0)
        hit = io_n == gi
        gsel = jnp.logical_or(gsel, hit)
        gscore = jnp.where(hit, neg, gscore)
    emask = jnp.broadcast_to(gsel.reshape(N_GROUPS, 1, tm), (N_GROUPS, GROUP_SIZE, tm)).reshape(N_EXPERTS, tm)
    val = jnp.where(emask, biased, neg)
    io_e = lax.broadcasted_iota(jnp.int32, (N_EXPERTS, tm), 0)
    idxs, ws, hits = [], [], []
    for _ in range(TOP_K):
        _, ei = _first_argmax(val, io_e, N_EXPERTS, 0)
        hit = io_e == ei
        ws.append(jnp.sum(jnp.where(hit, scores, 0.0), axis=0, keepdims=True))
        idxs.append(ei)
        hits.append(hit)
        val = jnp.where(hit, neg, val)
    w = jnp.concatenate(ws, axis=0)
    tile = pl.program_id(0) * pl.num_programs(1) + pl.program_id(1)
    valid = (tile * tm + lax.broadcasted_iota(jnp.int32, (1, tm), 1)) < t_real
    idx_all = jnp.where(valid, jnp.concatenate(idxs, axis=0), -1)
    gate_all = w / jnp.sum(w, axis=0, keepdims=True) * ROUTED_SCALE

    onehot = jnp.zeros((N_EXPERTS, tm), F32)
    for hit in hits:
        onehot = onehot + hit.astype(F32)
    onehot = jnp.where(valid, onehot, 0.0)
    tri = (lax.broadcasted_iota(jnp.int32, (tm, tm), 0) <= lax.broadcasted_iota(jnp.int32, (tm, tm), 1))
    csum = _dot(onehot.astype(BF16), tri.astype(BF16)) + carry[...]
    ranks = [jnp.sum(jnp.where(hit, csum, 0.0), axis=0, keepdims=True) - 1.0 for hit in hits]
    rank_all = jnp.concatenate(ranks, axis=0).astype(jnp.int32)
    for q in range(tm // LANES):
        rows = slice(q * TOP_K, (q + 1) * TOP_K)
        cols = slice(q * LANES, (q + 1) * LANES)
        idx_ref[rows, :] = idx_all[:, cols]
        gate_ref[rows, :] = gate_all[:, cols]
        rank_ref[rows, :] = rank_all[:, cols]
    counts = csum[:, tm - 1:tm]
    carry[...] = counts

    @pl.when(pl.program_id(1) == pl.num_programs(1) - 1)
    def _():
        r = float(MOE_R)
        padded = jnp.floor((counts + (r - 1.0)) / r) * r
        below = (lax.broadcasted_iota(jnp.int32, (N_EXPERTS, N_EXPERTS), 1)
                 < lax.broadcasted_iota(jnp.int32, (N_EXPERTS, N_EXPERTS), 0)).astype(F32)
        pstart = jnp.dot(below, jnp.broadcast_to(padded, (N_EXPERTS, LANES)), precision=lax.Precision.HIGHEST,
                         preferred_element_type=F32)[:, 0:1]
        pend = pstart + padded
        nb = be_ref.shape[-1]
        bpos = lax.broadcasted_iota(jnp.int32, (1, nb), 1).astype(F32) * r
        be = jnp.minimum(jnp.sum((bpos >= pend).astype(F32), axis=0, keepdims=True), float(N_EXPERTS - 1))
        sel = lax.broadcasted_iota(jnp.int32, (N_EXPERTS, nb), 0) == be.astype(jnp.int32)
        cnt_b = jnp.sum(jnp.where(sel, counts, 0.0), axis=0, keepdims=True)
        st_b = jnp.sum(jnp.where(sel, pstart, 0.0), axis=0, keepdims=True)
        ps_ref[0] = pstart.astype(jnp.int32)
        be_ref[0] = be.astype(jnp.int32)
        bn_ref[0] = jnp.clip(cnt_b - (bpos - st_b), 0.0, r).astype(jnp.int32)


def _router(x2, wr_t, rb, nseg, nblk_pad, t_real):
    t, d = x2.shape
    tm = TOK_TM
    tiles = t // (nseg * tm)
    tok_spec = pl.BlockSpec((tm // LANES * TOP_K, LANES), lambda s, j: (s * tiles + j, 0))
    tab_spec = pl.BlockSpec((1, 1, nblk_pad), lambda s, j: (s, 0, 0))
    tok_shape = (t // LANES * TOP_K, LANES)
    return pl.pallas_call(
        functools.partial(_router_kernel, t_real=t_real),
        grid=(nseg, tiles),
        in_specs=[pl.BlockSpec((tm, d), lambda s, j: (s * tiles + j, 0)), _full(wr_t.shape), _full(rb.shape)],
        out_specs=[tok_spec, tok_spec, tok_spec, pl.BlockSpec((1, N_EXPERTS, 1), lambda s, j: (s, 0, 0)),
                   tab_spec, tab_spec],
        out_shape=[jax.ShapeDtypeStruct(tok_shape, jnp.int32), jax.ShapeDtypeStruct(tok_shape, F32),
                   jax.ShapeDtypeStruct(tok_shape, jnp.int32),
                   jax.ShapeDtypeStruct((nseg, N_EXPERTS, 1), jnp.int32),
                   jax.ShapeDtypeStruct((nseg, 1, nblk_pad), jnp.int32),
                   jax.ShapeDtypeStruct((nseg, 1, nblk_pad), jnp.int32)],
        scratch_shapes=[pltpu.VMEM((N_EXPERTS, 1), F32)],
        compiler_params=_cp(("arbitrary", "arbitrary")),
        name="router",
    )(x2, wr_t, rb)


def _plan_sc(idx_f, rank_f, gate_f, pstart_f, nseg, tseg, nslots, lead):
    words = tseg * TOP_K
    chunk = words // 4
    assert chunk * 4 == words and chunk % 16 == 0 and nslots % 16 == 0
    info = plsc.get_sparse_core_info()
    ncores = info.num_cores
    nl = info.num_lanes
    mesh = plsc.VectorSubcoreMesh(core_axis_name="c", subcore_axis_name="s")
    dummy_off = tseg * ROW_SLAB

    def body(idx_hbm, rank_hbm, gate_hbm, ps_hbm, *rest):
        outs = rest[:3 * nseg]
        idx_v, rank_v, val_v, ps_v, slot_v = rest[3 * nseg:]
        wid = lax.axis_index("s") * ncores + lax.axis_index("c")
        lane = lax.iota(jnp.int32, nl)

        def run(seg, kind):
            want_gate = kind == 2
            out_hbm = outs[3 * seg + kind]
            pltpu.sync_copy(ps_hbm.at[pl.ds(seg * N_EXPERTS, N_EXPERTS)], ps_v)
            fill = jnp.full((nl,), dummy_off if kind == 1 else 0, jnp.int32)

            @pl.loop(0, nslots, step=nl)
            def _(i):
                slot_v[pl.ds(i, nl)] = fill

            @pl.loop(0, 4)
            def _(ci):
                base = seg * words + ci * chunk
                pltpu.sync_copy(idx_hbm.at[pl.ds(base, chunk)], idx_v)
                pltpu.sync_copy(rank_hbm.at[pl.ds(base, chunk)], rank_v)
                if want_gate:
                    pltpu.sync_copy(gate_hbm.at[pl.ds(base, chunk)], val_v)

                @pl.loop(0, chunk, step=nl)
                def _(o):
                    e = idx_v[pl.ds(o, nl)]
                    live = e >= 0
                    d = plsc.load_gather(ps_v, [jnp.maximum(e, 0)]) + rank_v[pl.ds(o, nl)] + lead
                    d = jnp.where(live, d, 0)
                    if want_gate:
                        plsc.store_scatter(slot_v, [d], val_v[pl.ds(o, nl)], mask=live)
                    else:
                        f = ci * chunk + o + lane
                        tok = ((f >> 10) << 7) | (f & (LANES - 1))
                        plsc.store_scatter(slot_v, [d], tok * ROW_SLAB, mask=live)

            pltpu.sync_copy(slot_v, out_hbm)

        for seg in range(nseg):
            for kind in range(3):
                @pl.when(wid == 3 * seg + kind)
                def _(seg=seg, kind=kind):
                    run(seg, kind)

    out_type = [jax.ShapeDtypeStruct((nslots,), jnp.int32) for _ in range(3 * nseg)]
    fn = pl.kernel(
        body, out_type=out_type, mesh=mesh,
        scratch_types=[pltpu.VMEM((chunk,), jnp.int32), pltpu.VMEM((chunk,), jnp.int32),
                       pltpu.VMEM((chunk,), jnp.int32), pltpu.VMEM((N_EXPERTS,), jnp.int32),
                       pltpu.VMEM((nslots,), jnp.int32)],
        compiler_params=pltpu.CompilerParams(needs_layout_passes=False), name="plan_sc")
    return fn(idx_f, rank_f, lax.bitcast_convert_type(gate_f, jnp.int32), pstart_f)


def _expert_kernel(be_ref, bn_ref, slotg_ref, slots_ref, x_ref, ge_ref, go_ref,
                   w1e_ref, w3e_ref, w2e_ref, w1o_ref, w3o_ref, w2o_ref, *rest):
    acc_ref, tile_a, tile_b, ys_a, ys_b = rest[-5:]
    _expert_body(bn_ref, slotg_ref, slots_ref, x_ref, (ge_ref, w1e_ref, w3e_ref, w2e_ref),
                 (go_ref, w1o_ref, w3o_ref, w2o_ref), acc_ref, tile_a, tile_b, ys_a, ys_b)
    del be_ref


def _expert_body(bn_ref, slotg_ref, slots_ref, x_ref, even_refs, odd_refs,
                 acc_ref, tile_a, tile_b, ys_a, ys_b):
    i = pl.program_id(0)
    r = MOE_R
    s = TILE_S
    zero_rows = 64 * ROW_SLAB

    group = 4

    def gather_moves(blk, tile):
        base = (blk + 1) * r

        def move(q):
            for row in range(q * group, (q + 1) * group):
                off = pl.multiple_of(slotg_ref[base + row], ROW_SLAB)
                tile[pl.ds(row, ROW_SLAB, stride=s), :] = x_ref[pl.ds(off, ROW_SLAB), :]

        return [functools.partial(move, q) for q in range(r // group)]

    def scatter_moves(blk, ys):
        base = (blk + 1) * r

        def move(q):
            offs, vals = [], []
            for row in range(q * group, (q + 1) * group):
                off = pl.multiple_of(slots_ref[base + row], ROW_SLAB)
                src = (row // SUBLANES) * (ROW_SLAB * SUBLANES) + row % SUBLANES
                vals.append(acc_ref[pl.ds(off, ROW_SLAB), :] + ys[pl.ds(src, ROW_SLAB, stride=SUBLANES), :])
                offs.append(off)
            for off, val in zip(offs, vals):
                acc_ref[pl.ds(off, ROW_SLAB), :] = val

        return [functools.partial(move, q) for q in range(r // group)]

    def ffn_stages(tile, refs, ys):
        g_ref, w1_ref, w3_ref, w2_ref = refs
        xb = jnp.concatenate([tile[c * s:c * s + r, :] for c in range(ROW_SLAB)], axis=-1).astype(BF16)
        h1 = _dot(xb, w1_ref[0])
        yield
        h3 = _dot(xb, w3_ref[0])
        yield
        gcol = jnp.transpose(jnp.broadcast_to(g_ref[0], (SUBLANES, r)))[:, 0:1]
        hh = (_silu(h1) * h3).astype(BF16)
        wide = 2 * LANES
        for nc in range(D_MODEL // wide):
            y = _dot(hh, w2_ref[0, :, nc * wide:(nc + 1) * wide]) * gcol
            for g in range(r // SUBLANES):
                for cc in range(wide // LANES):
                    c = nc * (wide // LANES) + cc
                    ys[pl.ds((g * ROW_SLAB + c) * SUBLANES, SUBLANES), :] = (
                        y[g * SUBLANES:(g + 1) * SUBLANES, cc * LANES:(cc + 1) * LANES])
            yield

    def run_phase(stages, moves):
        n_stage = 2 + D_MODEL // (2 * LANES)
        per = -(-len(moves) // n_stage)
        for k, _ in enumerate(stages):
            for mv in moves[k * per:(k + 1) * per]:
                mv()

    def run_all(moves):
        for mv in moves:
            mv()

    @pl.when(i == 0)
    def _():
        def zero(k, c):
            acc_ref[pl.ds(pl.multiple_of(k * zero_rows, zero_rows), zero_rows), :] = jnp.zeros((zero_rows, LANES), F32)
            return c

        lax.fori_loop(0, acc_ref.shape[0] // zero_rows, zero, 0)
        rem = acc_ref.shape[0] % zero_rows
        if rem:
            acc_ref[pl.ds(acc_ref.shape[0] - rem, rem), :] = jnp.zeros((rem, LANES), F32)
        tile_a[...] = jnp.zeros(tile_a.shape, F32)
        tile_b[...] = jnp.zeros(tile_b.shape, F32)
        ys_b[...] = jnp.zeros(ys_b.shape, F32)
        run_all(gather_moves(0, tile_a))

    even = 2 * i
    odd = even + 1
    pending = bn_ref[even] + bn_ref[odd] + bn_ref[jnp.maximum(even - 1, 0)]

    @pl.when(pending > 0)
    def _():
        def mixed(a, b):
            return [mv for pair in zip(a, b) for mv in pair]

        run_phase(ffn_stages(tile_a, even_refs, ys_a),
                  mixed(gather_moves(odd, tile_b), scatter_moves(even - 1, ys_b)))
        run_phase(ffn_stages(tile_b, odd_refs, ys_b),
                  mixed(gather_moves(even + 2, tile_a), scatter_moves(even, ys_a)))


def _expert_segment(si, nseg, blk_e, blk_n, slots_g, slots_s, gates3, xslab3, w1, w3, w2, tseg, nblk,
                    routed_prev):
    r = MOE_R
    acc_rows = (tseg + SUBLANES) * ROW_SLAB
    one = pl.Buffered(1)
    w1_blk = (1, D_MODEL, D_EXPERT)
    w2_blk = (1, D_EXPERT, D_MODEL)
    g_blk = (1, 1, r)
    in_specs = [pl.BlockSpec((None, tseg * ROW_SLAB, LANES), lambda i, *_: (si, 0, 0), pipeline_mode=one),
                pl.BlockSpec(g_blk, lambda i, *_: (2 * i + 1, 0, 0)),
                pl.BlockSpec(g_blk, lambda i, *_: (2 * i + 2, 0, 0)),
                pl.BlockSpec(w1_blk, lambda i, be, *_: (be[2 * i], 0, 0)),
                pl.BlockSpec(w1_blk, lambda i, be, *_: (be[2 * i], 0, 0)),
                pl.BlockSpec(w2_blk, lambda i, be, *_: (be[2 * i], 0, 0)),
                pl.BlockSpec(w1_blk, lambda i, be, *_: (be[2 * i + 1], 0, 0)),
                pl.BlockSpec(w1_blk, lambda i, be, *_: (be[2 * i + 1], 0, 0)),
                pl.BlockSpec(w2_blk, lambda i, be, *_: (be[2 * i + 1], 0, 0))]
    args = [blk_e, blk_n, slots_g, slots_s, xslab3, gates3, gates3, w1, w3, w2, w1, w3, w2]
    aliases = {}
    if routed_prev is not None:
        in_specs.append(pl.BlockSpec(memory_space=pl.ANY))
        aliases = {len(args): 0}
        args.append(routed_prev)
    grid_spec = pltpu.PrefetchScalarGridSpec(
        num_scalar_prefetch=4,
        grid=(nblk // 2 + 1,),
        in_specs=in_specs,
        out_specs=pl.BlockSpec((None, acc_rows, LANES), lambda i, *_: (si, 0, 0), pipeline_mode=one),
        scratch_shapes=[pltpu.VMEM((ROW_SLAB * TILE_S, LANES), F32),
                        pltpu.VMEM((ROW_SLAB * TILE_S, LANES), F32),
                        pltpu.VMEM((r * ROW_SLAB, LANES), F32),
                        pltpu.VMEM((r * ROW_SLAB, LANES), F32)],
    )
    return pl.pallas_call(
        _expert_kernel,
        grid_spec=grid_spec,
        out_shape=jax.ShapeDtypeStruct((nseg, acc_rows, LANES), F32),
        input_output_aliases=aliases,
        compiler_params=_cp(("arbitrary",)),
        name="experts",
    )(*args)


def _final_kernel(x_ref, r_ref, w13_ref, w2_ref, g_ref, b_ref, yp_ref, ys_ref, *, tiles, p_tiles):
    tm = x_ref.shape[0]
    x = x_ref[...]
    h13 = _dot(x.astype(BF16), w13_ref[...])
    hh = (_silu(h13[:, :D_EXPERT]) * h13[:, D_EXPERT:]).astype(BF16)
    sh = _dot(hh, w2_ref[...])
    routed = jnp.concatenate([r_ref[pl.ds(c, tm, stride=ROW_SLAB), :] for c in range(ROW_SLAB)], axis=-1)
    y = _ln(ALPHA * x + (sh + routed), g_ref[...], b_ref[...])
    tile = pl.program_id(0) * tiles + pl.program_id(1)

    @pl.when(tile < p_tiles)
    def _():
        yp_ref[...] = y

    @pl.when(tile >= p_tiles)
    def _():
        ys_ref[...] = y[:ys_ref.shape[0], :]


def _final(x2, routed3, w13, w2, g, b, n_prompt, n_sample):
    t, d = x2.shape
    nseg = routed3.shape[0]
    tm = TOK_TM
    tiles = t // (nseg * tm)
    p_tiles = n_prompt // tm
    assert p_tiles * tm == n_prompt and t - n_prompt == tm and n_sample <= tm
    return pl.pallas_call(
        functools.partial(_final_kernel, tiles=tiles, p_tiles=p_tiles),
        grid=(nseg, tiles),
        in_specs=[pl.BlockSpec((tm, d), lambda s, j: (s * tiles + j, 0)),
                  pl.BlockSpec((None, tm * ROW_SLAB, LANES), lambda s, j: (s, j, 0)),
                  _full(w13.shape), _full(w2.shape), _full(g.shape), _full(b.shape)],
        out_specs=[pl.BlockSpec((tm, d), lambda s, j: (jnp.minimum(s * tiles + j, p_tiles - 1), 0)),
                   pl.BlockSpec((n_sample, d), lambda s, j: (0, 0))],
        out_shape=[jax.ShapeDtypeStruct((n_prompt, d), F32), jax.ShapeDtypeStruct((n_sample, d), F32)],
        compiler_params=_cp(("arbitrary", "arbitrary")),
        name="final",
    )(x2, routed3, w13, w2, g, b)


def _block_diag_dense(w):
    h, dh, _ = w.shape
    eye = jnp.eye(h, dtype=w.dtype)
    return jnp.einsum('hij,hg->higj', w, eye).reshape(h * dh, h * dh)


def kernel(x_prompt, x_sample, mem_prompt, state_rglru_h, state_rglru_conv, state_dwconv, cache_mem_k, cache_mem_v, w_in, b_in, w_conv_a, b_conv_a, w_rg_a, b_rg_a, w_rg_x, b_rg_x, rg_lambda, w_conv_b, b_conv_b, ln_c_g, ln_c_b, w_out, ln1_g, ln1_b, w_q, w_k, w_v, w_o, ln2_g, ln2_b, w_router, router_bias, w1_e, w3_e, w2_e, w1_s, w3_s, w2_s, ln3_g, ln3_b):
    assert w_in.shape[0] == 1, "single-layer trunk"
    nb, L, d = x_prompt.shape
    ns = x_sample.shape[0]
    nm = mem_prompt.shape[1]
    row = lambda a: a.reshape(1, -1)

    wp = dict(
        w_in=w_in[0].astype(BF16), b_in=row(b_in[0]),
        w_conv_a=w_conv_a[0], b_conv_a=row(b_conv_a[0]),
        w_gate=jnp.stack([jnp.stack([_block_diag_dense(w[:REC_HEADS // 2]), _block_diag_dense(w[REC_HEADS // 2:])])
                          for w in (w_rg_a[0], w_rg_x[0])]).astype(BF16),
        b_gate=row(jnp.concatenate([b_rg_a[0], b_rg_x[0]])),
        lam=row(rg_lambda[0]),
        w_conv_b=w_conv_b[0], b_conv_b=row(b_conv_b[0]),
        ln_c_g=row(ln_c_g[0]), ln_c_b=row(ln_c_b[0]),
        w_out=w_out[0].astype(BF16), ln1_g=row(ln1_g[0]), ln1_b=row(ln1_b[0]),
    )
    wq = w_q[0].astype(BF16)
    wo = w_o[0].astype(BF16)
    wkv = jnp.concatenate([w_k[0], w_v[0]], axis=1).astype(BF16)
    l2g, l2b = row(ln2_g[0]), row(ln2_b[0])

    x1_p, h_p, ca_p, cb_p = _mix_prompt(x_prompt, wp)
    kv = _linear(mem_prompt.reshape(nb * nm, d), wkv, 512, "kv_proj")
    mk_p = kv[:, :d].reshape(nb, nm, d)
    mv_p = kv[:, d:].reshape(nb, nm, d)
    t_real = nb * L + ns
    t = -(-t_real // (MOE_NSEG * TOK_TM)) * (MOE_NSEG * TOK_TM)
    x2, x2_slab = _attn_prompt(x1_p, mk_p, mv_p, wq, wo, l2g, l2b, t)

    xs = x_sample.reshape(ns, d)
    sta = jnp.transpose(state_rglru_conv[0], (1, 0, 2))
    stb = jnp.transpose(state_dwconv[0], (1, 0, 2))
    x1_s, h_s, u_s, glu_s = _mix_sample(xs, sta, stb, state_rglru_h[0], wp)
    q_s = _linear(x1_s, wq, ns, "q_sample")
    o_s = _attn_sample(q_s, cache_mem_k, cache_mem_v)
    x2, x2_slab = _attn_out_sample(o_s, x1_s, wo, l2g, l2b, x2, x2_slab, nb * L)

    nseg = MOE_NSEG
    tseg = t // nseg
    assert tseg * nseg == t and tseg % LANES == 0 and t % TOK_TM == 0
    nblk = -(-(tseg * TOP_K + N_EXPERTS * (MOE_R - 1)) // MOE_R)
    nblk += nblk % 2
    nblk_pad = -(-(nblk + 2) // LANES) * LANES
    idx_f, gate_f, rank_f, pstart, blk_e, blk_n = _router(
        x2, jnp.transpose(w_router[0]), router_bias[0].reshape(N_EXPERTS, 1), nseg, nblk_pad, t_real)
    plan = _plan_sc(idx_f.reshape(-1), rank_f.reshape(-1), gate_f.reshape(-1), pstart.reshape(-1),
                    nseg, tseg, (nblk + 4) * MOE_R, MOE_R)
    w1_eb, w3_eb, w2_eb = w1_e[0].astype(BF16), w3_e[0].astype(BF16), w2_e[0].astype(BF16)
    xslab3 = x2_slab.reshape(nseg, tseg * ROW_SLAB, LANES)
    routed3 = None
    for si in range(nseg):
        gates3 = lax.bitcast_convert_type(plan[3 * si + 2], F32).reshape(nblk + 4, 1, MOE_R)
        routed3 = _expert_segment(si, nseg, blk_e[si, 0], blk_n[si, 0], plan[3 * si], plan[3 * si + 1], gates3,
                                  xslab3, w1_eb, w3_eb, w2_eb, tseg, nblk, routed3)
    w13_s = jnp.concatenate([w1_s[0], w3_s[0]], axis=-1).astype(BF16)
    y_p, y_s = _final(x2, routed3, w13_s, w2_s[0].astype(BF16), row(ln3_g[0]), row(ln3_b[0]), nb * L, ns)
    y_p = y_p.reshape(nb, L, d)
    y_s = y_s.reshape(ns, 1, d)
    new_ca_s = jnp.concatenate([state_rglru_conv[0][:, 1:], u_s[:, None, :]], axis=1)
    new_cb_s = jnp.concatenate([state_dwconv[0][:, 1:], glu_s[:, None, :]], axis=1)
    hd = XA_HEAD_DIM
    return (y_p, y_s, h_p[None], ca_p[None], cb_p[None],
            mk_p.reshape(1, nb, nm, XA_HEADS, hd), mv_p.reshape(1, nb, nm, XA_HEADS, hd),
            h_s[None], new_ca_s[None], new_cb_s[None])
```
